```python
import math
import jax, jax.numpy as jnp
from jax import lax
import numpy as np

D_MODEL = 1024
BATCH = 4
SEQ = 8192
DEPTH = 1
DEC_BATCH = 32
DEC_SEQ = 2048
PAST_LEN = 128

N_MEM = 256
DA_HEADS = 8
DA_HEAD_DIM = 64
DA_Q_WIDTH = DA_HEADS * 2 * DA_HEAD_DIM
DA_V_WIDTH = DA_HEADS * 2 * DA_HEAD_DIM
ROT_DIM = DA_HEAD_DIM // 4
ROPE_THETA = 500000.0
Q_BLOCK = 128
SUBLN_EPS = 1e-5
LRU_WIDTH = 1024
LRU_BLOCKS = 8
LRU_BLOCK_DIM = LRU_WIDTH // LRU_BLOCKS
CONV_WIDTH = 4
CONV_PAD_LEFT = 2
LRU_C = 8.0
IN_SPLITS = (DA_Q_WIDTH,
             2 * DA_Q_WIDTH,
             2 * DA_Q_WIDTH + DA_V_WIDTH,
             2 * DA_Q_WIDTH + DA_V_WIDTH + LRU_WIDTH,
             2 * DA_Q_WIDTH + DA_V_WIDTH + 2 * LRU_WIDTH,
             2 * DA_Q_WIDTH + DA_V_WIDTH + 2 * LRU_WIDTH + D_MODEL)
IN_WIDTH = 2 * DA_Q_WIDTH + DA_V_WIDTH + 2 * LRU_WIDTH + 2 * D_MODEL
XA_HEADS = 4
XA_HEAD_DIM = D_MODEL // XA_HEADS
D_FF = ((8 * D_MODEL + 3 * 256 - 1) // (3 * 256)) * 256
DEEPNORM_ALPHA = (2.0 * DEPTH) ** 0.25
DEEPNORM_BETA = (8.0 * DEPTH) ** -0.25
LN_EPS = 1e-5

kernel_name = "hybrid_diffattn_rglru_encoder"


def layer_norm(x, g, b):
    xf = x.astype(jnp.float32)
    mu = jnp.mean(xf, axis=-1, keepdims=True)
    xc = xf - mu
    var = jnp.mean(xc * xc, axis=-1, keepdims=True)
    y = xc * lax.rsqrt(var + LN_EPS) * g.astype(jnp.float32) + b.astype(jnp.float32)
    return y.astype(x.dtype)


def rope_tables(T):
    inv = ROPE_THETA ** (-jnp.arange(0, ROT_DIM, 2, dtype=jnp.float32) / ROT_DIM)
    ang = jnp.arange(T, dtype=jnp.float32)[:, None] * inv[None, :]
    return jnp.cos(ang), jnp.sin(ang)


def apply_partial_rope(x, cos, sin):
    xr = x[..., :ROT_DIM].astype(jnp.float32)
    half = ROT_DIM // 2
    x1, x2 = xr[..., :half], xr[..., half:]
    c = cos[None, :, None, None, :]
    s = sin[None, :, None, None, :]
    rot = jnp.concatenate([x1 * c - x2 * s, x2 * c + x1 * s], axis=-1).astype(x.dtype)
    return jnp.concatenate([rot, x[..., ROT_DIM:]], axis=-1)


def diff_attention(q, k, v, lam):
    B, T = q.shape[0], q.shape[1]
    nblk = T // Q_BLOCK
    scale = DA_HEAD_DIM ** -0.5
    qb = (q * scale).reshape(B, nblk, Q_BLOCK, DA_HEADS, 2, DA_HEAD_DIM).transpose(1, 0, 2, 3, 4, 5)

    def one_block(qblk):
        s = jnp.einsum('bqhcd,bkhcd->bhcqk', qblk, k).astype(jnp.float32)
        p = jax.nn.softmax(s, axis=-1)
        p = p[:, :, 0] - lam * p[:, :, 1]
        return jnp.einsum('bhqk,bkhe->bqhe', p.astype(v.dtype), v)

    o = lax.map(one_block, qb)
    return o.transpose(1, 0, 2, 3, 4).reshape(B, T, DA_HEADS, 2 * DA_HEAD_DIM)


def centred_conv(x, w, b):
    T = x.shape[1]
    xp = jnp.pad(x, ((0, 0), (CONV_PAD_LEFT, CONV_WIDTH - 1 - CONV_PAD_LEFT), (0, 0)))
    out = xp[:, 0:T] * w[0]
    for j in range(1, CONV_WIDTH):
        out = out + xp[:, j:j + T] * w[j]
    return out + b


def block_diag(x, w, b):
    xb = x.reshape(x.shape[:-1] + (LRU_BLOCKS, LRU_BLOCK_DIM))
    y = jnp.einsum('btnd,nde->btne', xb, w).reshape(x.shape)
    return y + b


def lin_scan_combine(c1, c2):
    a1, b1 = c1
    a2, b2 = c2
    return a1 * a2, a2 * b1 + b2


def rg_lru_direction(x, w_a, b_a, w_x, b_x, a_param, reverse):
    T = x.shape[1]
    r = jax.nn.sigmoid(block_diag(x, w_a, b_a).astype(jnp.float32))
    i = jax.nn.sigmoid(block_diag(x, w_x, b_x).astype(jnp.float32))
    log_a = -LRU_C * r * jax.nn.softplus(-a_param.astype(jnp.float32))
    a = jnp.exp(log_a)
    mult = jnp.sqrt(-jnp.expm1(2.0 * log_a))
    start = T - 1 if reverse else 0
    is_start = (jnp.arange(T) == start)[None, :, None]
    mult = jnp.where(is_start, 1.0, mult)
    u = mult * i * x.astype(jnp.float32)
    _, h = lax.associative_scan(lin_scan_combine, (a, u), reverse=reverse, axis=1)
    return h


def mixer(x, w_in, lambda_q1, lambda_k1, lambda_q2, lambda_k2, subln_g, conv_w, conv_b,
          lru_wa, lru_ba, lru_wx, lru_bx, lru_a, p_attn, p_lru, w_mix_out, lambda_init):
    B, T, _ = x.shape
    proj = x @ w_in
    q, k, v, xr, yr, g_attn, g_lru = jnp.split(proj, IN_SPLITS, axis=-1)

    cos, sin = rope_tables(T)
    q = apply_partial_rope(q.reshape(B, T, DA_HEADS, 2, DA_HEAD_DIM), cos, sin)
    k = apply_partial_rope(k.reshape(B, T, DA_HEADS, 2, DA_HEAD_DIM), cos, sin)
    v = v.reshape(B, T, DA_HEADS, 2 * DA_HEAD_DIM)
    f32 = jnp.float32
    lam = (jnp.exp(jnp.sum(lambda_q1.astype(f32) * lambda_k1.astype(f32)))
           - jnp.exp(jnp.sum(lambda_q2.astype(f32) * lambda_k2.astype(f32))) + lambda_init)
    o = diff_attention(q, k, v, lam).astype(f32)
    o = o * lax.rsqrt(jnp.mean(o * o, axis=-1, keepdims=True) + SUBLN_EPS) * subln_g.astype(f32)
    attn_out = (o * (1.0 - lambda_init)).reshape(B, T, DA_V_WIDTH).astype(x.dtype)

    xc = centred_conv(xr, conv_w, conv_b)
    h = (rg_lru_direction(xc, lru_wa[0], lru_ba[0], lru_wx[0], lru_bx[0], lru_a[0], False)
         + rg_lru_direction(xc, lru_wa[1], lru_ba[1], lru_wx[1], lru_bx[1], lru_a[1], True))
    lru_out = (h * jax.nn.gelu(yr.astype(f32))).astype(x.dtype)

    merged = jax.nn.sigmoid(g_attn) * (attn_out @ p_attn) + jax.nn.sigmoid(g_lru) * (lru_out @ p_lru)
    return merged @ w_mix_out


def cross_attention(x, mem, xa_wq, xa_wkv, xa_wo):
    B, T, _ = x.shape
    M = mem.shape[1]
    q = (x @ xa_wq).reshape(B, T, XA_HEADS, XA_HEAD_DIM) * (XA_HEAD_DIM ** -0.5)
    k, v = jnp.split(mem @ xa_wkv, 2, axis=-1)
    k = k.reshape(B, M, XA_HEADS, XA_HEAD_DIM)
    v = v.reshape(B, M, XA_HEADS, XA_HEAD_DIM)
    s = jnp.einsum('bqhd,bkhd->bhqk', q, k).astype(jnp.float32)
    p = jax.nn.softmax(s, axis=-1)
    o = jnp.einsum('bhqk,bkhd->bqhd', p.astype(v.dtype), v).reshape(B, T, D_MODEL)
    return o @ xa_wo


def swiglu(x, ffn_w_in, ffn_w_out):
    g, u = jnp.split(x @ ffn_w_in, 2, axis=-1)
    return (jax.nn.silu(g) * u) @ ffn_w_out


def encoder_layer(x, mem, layer_idx, w_in, lambda_q1, lambda_k1, lambda_q2, lambda_k2, subln_g,
                  conv_w, conv_b, lru_wa, lru_ba, lru_wx, lru_bx, lru_a, p_attn, p_lru, w_mix_out,
                  ln1_g, ln1_b, xa_wq, xa_wkv, xa_wo, ln2_g, ln2_b, ffn_w_in, ffn_w_out, ln3_g, ln3_b):
    lambda_init = 0.8 - 0.6 * math.exp(-0.3 * layer_idx)
    m = mixer(x, w_in, lambda_q1, lambda_k1, lambda_q2, lambda_k2, subln_g, conv_w, conv_b,
              lru_wa, lru_ba, lru_wx, lru_bx, lru_a, p_attn, p_lru, w_mix_out, lambda_init)
    x = layer_norm(DEEPNORM_ALPHA * x + m, ln1_g, ln1_b)
    x = layer_norm(DEEPNORM_ALPHA * x + cross_attention(x, mem, xa_wq, xa_wkv, xa_wo), ln2_g, ln2_b)
    x = layer_norm(DEEPNORM_ALPHA * x + swiglu(x, ffn_w_in, ffn_w_out), ln3_g, ln3_b)
    return x


def trunk(x, mem, weights):
    for l in range(DEPTH):
        layer_w = [w[l] for w in weights]
        x = encoder_layer(x, mem, l, *layer_w)
    return x


def setup_inputs(seed: int = 0) -> dict:
    key = jax.random.key(seed)
    ks = jax.random.split(key, 40)
    f32 = jnp.float32

    def nrm(k, shape, scale):
        return jax.random.normal(k, shape, f32) * scale

    def gain(k, shape):
        return 1.0 + 0.02 * jax.random.normal(k, shape, f32)

    a0 = jax.random.uniform(ks[16], (DEPTH, 2, LRU_WIDTH), f32, 0.9, 0.999)
    s0 = a0 ** (1.0 / LRU_C)
    lru_a = jnp.log(s0) - jnp.log1p(-s0)

    return {
        "x_prompt": nrm(ks[0], (BATCH, SEQ, D_MODEL), 1.0),
        "x_sample": nrm(ks[1], (DEC_BATCH, DEC_SEQ, D_MODEL), 1.0),
        "mem_prompt": nrm(ks[2], (BATCH, N_MEM, D_MODEL), 1.0),
        "mem_sample": nrm(ks[3], (DEC_BATCH, N_MEM, D_MODEL), 1.0),
        "w_in": nrm(ks[4], (DEPTH, D_MODEL, IN_WIDTH), D_MODEL ** -0.5),
        "lambda_q1": nrm(ks[5], (DEPTH, DA_HEAD_DIM), 0.1),
        "lambda_k1": nrm(ks[6], (DEPTH, DA_HEAD_DIM), 0.1),
        "lambda_q2": nrm(ks[7], (DEPTH, DA_HEAD_DIM), 0.1),
        "lambda_k2": nrm(ks[8], (DEPTH, DA_HEAD_DIM), 0.1),
        "subln_g": gain(ks[9], (DEPTH, 2 * DA_HEAD_DIM)),
        "conv_w": nrm(ks[10], (DEPTH, CONV_WIDTH, LRU_WIDTH), CONV_WIDTH ** -0.5),
        "conv_b": nrm(ks[11], (DEPTH, LRU_WIDTH), 0.02),
        "lru_wa": nrm(ks[12], (DEPTH, 2, LRU_BLOCKS, LRU_BLOCK_DIM, LRU_BLOCK_DIM), LRU_BLOCK_DIM ** -0.5),
        "lru_ba": nrm(ks[13], (DEPTH, 2, LRU_WIDTH), 0.02),
        "lru_wx": nrm(ks[14], (DEPTH, 2, LRU_BLOCKS, LRU_BLOCK_DIM, LRU_BLOCK_DIM), LRU_BLOCK_DIM ** -0.5),
        "lru_bx": nrm(ks[15], (DEPTH, 2, LRU_WIDTH), 0.02),
        "lru_a": lru_a,
        "p_attn": nrm(ks[17], (DEPTH, DA_V_WIDTH, D_MODEL), DA_V_WIDTH ** -0.5),
        "p_lru": nrm(ks[18], (DEPTH, LRU_WIDTH, D_MODEL), LRU_WIDTH ** -0.5),
        "w_mix_out": nrm(ks[19], (DEPTH, D_MODEL, D_MODEL), DEEPNORM_BETA * D_MODEL ** -0.5),
        "ln1_g": gain(ks[20], (DEPTH, D_MODEL)),
        "ln1_b": nrm(ks[21], (DEPTH, D_MODEL), 0.02),
        "xa_wq": nrm(ks[22], (DEPTH, D_MODEL, D_MODEL), D_MODEL ** -0.5),
        "xa_wkv": nrm(ks[23], (DEPTH, D_MODEL, 2 * D_MODEL), D_MODEL ** -0.5),
        "xa_wo": nrm(ks[24], (DEPTH, D_MODEL, D_MODEL), DEEPNORM_BETA * D_MODEL ** -0.5),
        "ln2_g": gain(ks[25], (DEPTH, D_MODEL)),
        "ln2_b": nrm(ks[26], (DEPTH, D_MODEL), 0.02),
        "ffn_w_in": nrm(ks[27], (DEPTH, D_MODEL, 2 * D_FF), D_MODEL ** -0.5),
        "ffn_w_out": nrm(ks[28], (DEPTH, D_FF, D_MODEL), DEEPNORM_BETA * D_FF ** -0.5),
        "ln3_g": gain(ks[29], (DEPTH, D_MODEL)),
        "ln3_b": nrm(ks[30], (DEPTH, D_MODEL), 0.02),
    }


def reference(x_prompt, x_sample, mem_prompt, mem_sample, w_in, lambda_q1, lambda_k1, lambda_q2,
              lambda_k2, subln_g, conv_w, conv_b, lru_wa, lru_ba, lru_wx, lru_bx, lru_a, p_attn, p_lru,
              w_mix_out, ln1_g, ln1_b, xa_wq, xa_wkv, xa_wo, ln2_g, ln2_b, ffn_w_in, ffn_w_out,
              ln3_g, ln3_b):
    weights = (w_in, lambda_q1, lambda_k1, lambda_q2, lambda_k2, subln_g, conv_w, conv_b,
               lru_wa, lru_ba, lru_wx, lru_bx, lru_a, p_attn, p_lru, w_mix_out,
               ln1_g, ln1_b, xa_wq, xa_wkv, xa_wo, ln2_g, ln2_b, ffn_w_in, ffn_w_out, ln3_g, ln3_b)
    y_prompt = trunk(x_prompt, mem_prompt, weights)
    y_sample = trunk(x_sample, mem_sample, weights)
    return (y_prompt, y_sample)
```

```python
import functools
import math

import jax
import jax.numpy as jnp
from jax import lax
from jax.experimental import pallas as pl
from jax.experimental.pallas import tpu as pltpu

F32 = jnp.float32
BF16 = jnp.bfloat16

D_MODEL = 1024
N_MEM = 256
DA_HEADS = 8
DA_HEAD_DIM = 64
DA_PAIR = 2 * DA_HEAD_DIM
ROT_DIM = DA_HEAD_DIM // 4
ROT_HALF = ROT_DIM // 2
ROPE_THETA = 500000.0
SUBLN_EPS = 1e-5
LRU_BLOCKS = 8
LRU_BLOCK_DIM = D_MODEL // LRU_BLOCKS
CONV_WIDTH = 4
LRU_C = 8.0
N_SEG = 7
XA_HEADS = 4
XA_HEAD_DIM = D_MODEL // XA_HEADS
D_FF = ((8 * D_MODEL + 3 * 256 - 1) // (3 * 256)) * 256
DEPTH = 1
DEEPNORM_ALPHA = (2.0 * DEPTH) ** 0.25
LN_EPS = 1e-5
LAMBDA_INIT = 0.8 - 0.6 * math.exp(-0.3 * 0)

LANES_V7X = 128
SUBLANES_V7X = 8
VMEM_LIMIT_V7X = 56 * 1024 * 1024

PROJ_ROWS = 1024
LRU_CHUNK = 512
ATTN_TQ = 512
ATTN_TK = 512
ROW_TILE = 512
FFN_CHUNKS = ((0, 1024), (1024, 1024), (2048, D_FF - 2048))
NEG_BIG = -1e30


def _params(*sem):
    return pltpu.CompilerParams(dimension_semantics=sem, vmem_limit_bytes=VMEM_LIMIT_V7X)


def _const_spec(shape):
    nd = len(shape)
    return pl.BlockSpec(shape, lambda *_: (0,) * nd, pipeline_mode=pl.Buffered(1))


def _layer_norm(x, g, b):
    mu = jnp.mean(x, axis=-1, keepdims=True)
    xc = x - mu
    var = jnp.mean(xc * xc, axis=-1, keepdims=True)
    return xc * lax.rsqrt(var + LN_EPS) * g + b


def _proj_kernel(x_ref, w_ref, cos_ref, sa_ref, sb_ref, qkv_ref, rest_ref, xb_ref):
    j = pl.program_id(1)

    @pl.when(j == 0)
    def _():
        xb_ref[...] = x_ref[...].astype(BF16)

    def matmul():
        return jnp.dot(xb_ref[...], w_ref[...], preferred_element_type=F32)

    def rope_store(acc, scale):
        c, sa, sb = cos_ref[...], sa_ref[...], sb_ref[...]
        for cb in range(D_MODEL // LANES_V7X):
            sl = slice(cb * LANES_V7X, (cb + 1) * LANES_V7X)
            blk = acc[:, sl]
            r = (blk * c + pltpu.roll(blk, ROT_HALF, 1) * sa
                 + pltpu.roll(blk, LANES_V7X - ROT_HALF, 1) * sb)
            qkv_ref[:, sl] = (r * scale).astype(BF16)

    @pl.when(j == 0)
    def _():
        rope_store(matmul(), DA_HEAD_DIM ** -0.5)

    @pl.when(j == 1)
    def _():
        rope_store(matmul(), 1.0)

    @pl.when(j == 2)
    def _():
        qkv_ref[...] = matmul().astype(BF16)

    @pl.when(j >= 3)
    def _():
        rest_ref[...] = matmul()


def _proj(x2d, w_in, cos_t, sa_t, sb_t, T):
    M = x2d.shape[0]
    tm = min(PROJ_ROWS, T)
    tpb = T // tm
    return pl.pallas_call(
        _proj_kernel,
        grid=(M // tm, N_SEG),
        in_specs=[
            pl.BlockSpec((tm, D_MODEL), lambda i, j: (i, 0)),
            pl.BlockSpec((D_MODEL, D_MODEL), lambda i, j: (0, j)),
            pl.BlockSpec((tm, LANES_V7X), lambda i, j: (i % tpb, 0)),
            pl.BlockSpec((tm, LANES_V7X), lambda i, j: (i % tpb, 0)),
            pl.BlockSpec((tm, LANES_V7X), lambda i, j: (i % tpb, 0)),
        ],
        out_specs=[
            pl.BlockSpec((tm, D_MODEL), lambda i, j: (i, jnp.minimum(j, 2))),
            pl.BlockSpec((tm, D_MODEL), lambda i, j: (i, jnp.maximum(j - 3, 0))),
        ],
        out_shape=[
            jax.ShapeDtypeStruct((M, 3 * D_MODEL), BF16),
            jax.ShapeDtypeStruct((M, 4 * D_MODEL), F32),
        ],
        scratch_shapes=[pltpu.VMEM((tm, D_MODEL), BF16)],
        compiler_params=_params("arbitrary", "arbitrary"),
        name="proj",
    )(x2d, w_in, cos_t, sa_t, sb_t)


def _lru_kernel(xf_ref, xfp_ref, xfn_ref, xr_ref, xrp_ref, xrn_ref, cw_ref, cb_ref, w_ref,
                bias_ref, la_ref, hf_ref, hb_ref, af_ref, ab_ref, carry_ref, *, tc):
    c = pl.program_id(1)
    n_c = pl.num_programs(1)

    @pl.when(c == 0)
    def _():
        carry_ref[...] = jnp.zeros_like(carry_ref)

    rows = lax.broadcasted_iota(jnp.int32, (tc, LRU_BLOCK_DIM), 0)
    n_ext = tc + 2 * SUBLANES_V7X

    def prepare(x_ref, xp_ref, xn_ref, d, has_prev, has_next, start_mask, a_ref, u_ref):
        prev = jnp.where(has_prev, xp_ref[0], 0.0)
        nxt = jnp.where(has_next, xn_ref[0], 0.0)
        xall = jnp.concatenate([prev, x_ref[0], nxt], axis=0)
        conv = (cw_ref[0:1, :] * pltpu.roll(xall, 2, 0)
                + cw_ref[1:2, :] * pltpu.roll(xall, 1, 0)
                + cw_ref[2:3, :] * xall
                + cw_ref[3:4, :] * pltpu.roll(xall, n_ext - 1, 0))
        xc = conv[SUBLANES_V7X:SUBLANES_V7X + tc] + cb_ref[...]
        xcb = xc.astype(BF16)
        lam = la_ref[d:d + 1, :]
        decay = -LRU_C * (jnp.maximum(-lam, 0.0) + jnp.log1p(jnp.exp(-jnp.abs(lam))))
        for n in range(LRU_BLOCKS):
            sl = slice(n * LRU_BLOCK_DIM, (n + 1) * LRU_BLOCK_DIM)
            y = jnp.dot(xcb[:, sl], w_ref[d, n], preferred_element_type=F32)
            r = jax.nn.sigmoid(y[:, :LRU_BLOCK_DIM] + bias_ref[d:d + 1, sl])
            gate = jax.nn.sigmoid(
                y[:, LRU_BLOCK_DIM:]
                + bias_ref[d:d + 1, D_MODEL + n * LRU_BLOCK_DIM:D_MODEL + (n + 1) * LRU_BLOCK_DIM])
            a = jnp.exp(r * decay[:, sl])
            mult = jnp.where(start_mask, 1.0, jnp.sqrt(1.0 - a * a))
            a_ref[:, sl] = a
            u_ref[0, :, sl] = mult * gate * xc[:, sl]

    first = c == 0
    last = c == n_c - 1
    prepare(xf_ref, xfp_ref, xfn_ref, 0, jnp.logical_not(first), jnp.logical_not(last),
            jnp.logical_and(first, rows == 0), af_ref, hf_ref)
    prepare(xr_ref, xrp_ref, xrn_ref, 1, jnp.logical_not(last), jnp.logical_not(first),
            jnp.logical_and(first, rows == tc - 1), ab_ref, hb_ref)

    def step(t, carry):
        hf, hb = carry
        hf = af_ref[pl.ds(t, 1), :] * hf + hf_ref[0, pl.ds(t, 1), :]
        hf_ref[0, pl.ds(t, 1), :] = hf
        tb = tc - 1 - t
        hb = ab_ref[pl.ds(tb, 1), :] * hb + hb_ref[0, pl.ds(tb, 1), :]
        hb_ref[0, pl.ds(tb, 1), :] = hb
        return hf, hb

    hf, hb = lax.fori_loop(0, tc, step, (carry_ref[0:1, :], carry_ref[1:2, :]), unroll=8)
    carry_ref[0:1, :] = hf
    carry_ref[1:2, :] = hb


def _lru(rest3, conv_w, conv_b, w_gates, b_gates, lru_a):
    B, T, _ = rest3.shape
    tc = min(LRU_CHUNK, T)
    n_c = T // tc
    hb8 = tc // SUBLANES_V7X
    n8 = T // SUBLANES_V7X
    seg = 3 - 3

    def cur(f):
        return pl.BlockSpec((1, tc, D_MODEL), lambda b, c: (b, f(c, n_c), seg))

    def prev(f):
        return pl.BlockSpec((1, SUBLANES_V7X, D_MODEL),
                            lambda b, c: (b, jnp.maximum(f(c, n_c) * hb8 - 1, 0), seg))

    def nxt(f):
        return pl.BlockSpec((1, SUBLANES_V7X, D_MODEL),
                            lambda b, c: (b, jnp.minimum((f(c, n_c) + 1) * hb8, n8 - 1), seg))

    fwd = lambda c, n: c
    bwd = lambda c, n: n - 1 - c
    return pl.pallas_call(
        functools.partial(_lru_kernel, tc=tc),
        grid=(B, n_c),
        in_specs=[
            cur(fwd), prev(fwd), nxt(fwd), cur(bwd), prev(bwd), nxt(bwd),
            _const_spec(conv_w.shape), _const_spec(conv_b.shape), _const_spec(w_gates.shape),
            _const_spec(b_gates.shape), _const_spec(lru_a.shape),
        ],
        out_specs=[
            pl.BlockSpec((1, tc, D_MODEL), lambda b, c: (b, c, 0)),
            pl.BlockSpec((1, tc, D_MODEL), lambda b, c: (b, n_c - 1 - c, 0)),
        ],
        out_shape=[jax.ShapeDtypeStruct((B, T, D_MODEL), F32)] * 2,
        scratch_shapes=[
            pltpu.VMEM((tc, D_MODEL), F32),
            pltpu.VMEM((tc, D_MODEL), F32),
            pltpu.VMEM((2, D_MODEL), F32),
        ],
        compiler_params=_params("arbitrary", "arbitrary"),
        name="lru",
    )(rest3, rest3, rest3, rest3, rest3, rest3, conv_w, conv_b, w_gates, b_gates, lru_a)


def _attn_kernel(lam_ref, g_ref, q_ref, k_ref, v_ref, o_ref, *, n_kv, tk):
    q = q_ref[0].astype(F32)
    lane = lax.broadcasted_iota(jnp.int32, q.shape, 1)
    qa = jnp.where(lane < DA_HEAD_DIM, q, 0.0).astype(BF16)
    qb = jnp.where(lane >= DA_HEAD_DIM, q, 0.0).astype(BF16)
    tq = q.shape[0]

    def online(qc, ks, vs, m, l, acc):
        s = lax.dot_general(qc, ks, (((1,), (1,)), ((), ())), preferred_element_type=F32)
        m_new = jnp.maximum(m, jnp.max(s, axis=1, keepdims=True))
        alpha = jnp.exp(m - m_new)
        p = jnp.exp(s - m_new)
        l = alpha * l + jnp.sum(p, axis=1, keepdims=True)
        acc = alpha * acc + jnp.dot(p.astype(BF16), vs, preferred_element_type=F32)
        return m_new, l, acc

    def body(c, carry):
        ma, la, acca, mb, lb, accb = carry
        off = pl.multiple_of(c * tk, tk)
        ks = k_ref[0, pl.ds(off, tk), :]
        vs = v_ref[0, pl.ds(off, tk), :]
        ma, la, acca = online(qa, ks, vs, ma, la, acca)
        mb, lb, accb = online(qb, ks, vs, mb, lb, accb)
        return ma, la, acca, mb, lb, accb

    m0 = jnp.full((tq, 1), NEG_BIG, F32)
    l0 = jnp.zeros((tq, 1), F32)
    acc0 = jnp.zeros((tq, DA_PAIR), F32)
    _, la, acca, _, lb, accb = lax.fori_loop(0, n_kv, body, (m0, l0, acc0, m0, l0, acc0))

    lam = (jnp.exp(jnp.sum(lam_ref[0:1, :] * lam_ref[1:2, :], axis=-1, keepdims=True))
           - jnp.exp(jnp.sum(lam_ref[2:3, :] * lam_ref[3:4, :], axis=-1, keepdims=True))
           + LAMBDA_INIT)
    o = acca / la - lam * (accb / lb)
    o = o * lax.rsqrt(jnp.mean(o * o, axis=-1, keepdims=True) + SUBLN_EPS) * g_ref[...]
    o_ref[0] = (o * (1.0 - LAMBDA_INIT)).astype(BF16)


def _attention(qkv3, lam4, subln_g):
    B, T, _ = qkv3.shape
    tq = min(ATTN_TQ, T)
    tk = min(ATTN_TK, T)
    return pl.pallas_call(
        functools.partial(_attn_kernel, n_kv=T // tk, tk=tk),
        grid=(B, DA_HEADS, T // tq),
        in_specs=[
            _const_spec(lam4.shape), _const_spec(subln_g.shape),
            pl.BlockSpec((1, tq, DA_PAIR), lambda b, h, i: (b, i, h)),
            pl.BlockSpec((1, T, DA_PAIR), lambda b, h, i: (b, 0, DA_HEADS + h)),
            pl.BlockSpec((1, T, DA_PAIR), lambda b, h, i: (b, 0, 2 * DA_HEADS + h)),
        ],
        out_specs=pl.BlockSpec((1, tq, DA_PAIR), lambda b, h, i: (b, i, h)),
        out_shape=jax.ShapeDtypeStruct((B, T, D_MODEL), BF16),
        compiler_params=_params("arbitrary", "arbitrary", "arbitrary"),
        name="diff_attn",
    )(lam4, subln_g, qkv3, qkv3, qkv3)


def _merge_kernel(attn_ref, hf_ref, hb_ref, yr_ref, ga_ref, gl_ref, x_ref, pa_ref, plru_ref,
                  wo_ref, g_ref, b_ref, o_ref):
    a_proj = jnp.dot(attn_ref[...], pa_ref[...], preferred_element_type=F32)
    lru_out = ((hf_ref[...] + hb_ref[...]) * jax.nn.gelu(yr_ref[...])).astype(BF16)
    l_proj = jnp.dot(lru_out, plru_ref[...], preferred_element_type=F32)
    merged = jax.nn.sigmoid(ga_ref[...]) * a_proj + jax.nn.sigmoid(gl_ref[...]) * l_proj
    m = jnp.dot(merged.astype(BF16), wo_ref[...], preferred_element_type=F32)
    o_ref[...] = _layer_norm(DEEPNORM_ALPHA * x_ref[...] + m, g_ref[...], b_ref[...])


def _merge(attn2, hf2, hb2, rest2, x2d, p_attn, p_lru, w_mix_out, ln_g, ln_b):
    M = x2d.shape[0]
    tm = ROW_TILE
    row = lambda seg: pl.BlockSpec((tm, D_MODEL), lambda i: (i, seg))
    return pl.pallas_call(
        _merge_kernel,
        grid=(M // tm,),
        in_specs=[
            row(0), row(0), row(0), row(1), row(2), row(3), row(0),
            _const_spec(p_attn.shape), _const_spec(p_lru.shape), _const_spec(w_mix_out.shape),
            _const_spec(ln_g.shape), _const_spec(ln_b.shape),
        ],
        out_specs=row(0),
        out_shape=jax.ShapeDtypeStruct((M, D_MODEL), F32),
        compiler_params=_params("arbitrary"),
        name="merge_ln1",
    )(attn2, hf2, hb2, rest2, rest2, rest2, x2d, p_attn, p_lru, w_mix_out, ln_g, ln_b)


def _memkv_kernel(m_ref, w_ref, o_ref):
    o_ref[...] = jnp.dot(m_ref[...].astype(BF16), w_ref[...],
                         preferred_element_type=F32).astype(BF16)


def _memkv(mem2, xa_wkv):
    M = mem2.shape[0]
    tm = ROW_TILE
    return pl.pallas_call(
        _memkv_kernel,
        grid=(M // tm, 2),
        in_specs=[
            pl.BlockSpec((tm, D_MODEL), lambda i, j: (i, 0)),
            pl.BlockSpec((D_MODEL, D_MODEL), lambda i, j: (0, j)),
        ],
        out_specs=pl.BlockSpec((tm, D_MODEL), lambda i, j: (i, j)),
        out_shape=jax.ShapeDtypeStruct((M, 2 * D_MODEL), BF16),
        compiler_params=_params("arbitrary", "arbitrary"),
        name="mem_kv",
    )(mem2, xa_wkv)


def _tail_kernel(x_ref, kv_ref, wq_ref, wo_ref, g2_ref, b2_ref, wi_ref, wf_ref, g3_ref, b3_ref,
                 o_ref):
    x1 = x_ref[...]
    q = (jnp.dot(x1.astype(BF16), wq_ref[...], preferred_element_type=F32)
         * (XA_HEAD_DIM ** -0.5)).astype(BF16)
    heads = []
    for h in range(XA_HEADS):
        sl = slice(h * XA_HEAD_DIM, (h + 1) * XA_HEAD_DIM)
        kh = kv_ref[0, :, sl]
        vh = kv_ref[0, :, D_MODEL + h * XA_HEAD_DIM:D_MODEL + (h + 1) * XA_HEAD_DIM]
        s = lax.dot_general(q[:, sl], kh, (((1,), (1,)), ((), ())), preferred_element_type=F32)
        e = jnp.exp(s - jnp.max(s, axis=-1, keepdims=True))
        p = e / jnp.sum(e, axis=-1, keepdims=True)
        heads.append(jnp.dot(p.astype(BF16), vh, preferred_element_type=F32).astype(BF16))
    xa = jnp.dot(jnp.concatenate(heads, axis=-1), wo_ref[...], preferred_element_type=F32)
    x2 = _layer_norm(DEEPNORM_ALPHA * x1 + xa, g2_ref[...], b2_ref[...])
    x2b = x2.astype(BF16)
    y = None
    for c0, cw in FFN_CHUNKS:
        g = jnp.dot(x2b, wi_ref[:, c0:c0 + cw], preferred_element_type=F32)
        u = jnp.dot(x2b, wi_ref[:, D_FF + c0:D_FF + c0 + cw], preferred_element_type=F32)
        hcb = (g * jax.nn.sigmoid(g) * u).astype(BF16)
        part = jnp.dot(hcb, wf_ref[c0:c0 + cw, :], preferred_element_type=F32)
        y = part if y is None else y + part
    o_ref[...] = _layer_norm(DEEPNORM_ALPHA * x2 + y, g3_ref[...], b3_ref[...])


def _tail(x1, kv3, T, xa_wq, xa_wo, ln2_g, ln2_b, ffn_w_in, ffn_w_out, ln3_g, ln3_b):
    M = x1.shape[0]
    tm = ROW_TILE
    tpb = T // tm
    row = pl.BlockSpec((tm, D_MODEL), lambda i: (i, 0))
    return pl.pallas_call(
        _tail_kernel,
        grid=(M // tm,),
        in_specs=[
            row,
            pl.BlockSpec((1, N_MEM, 2 * D_MODEL), lambda i: (i // tpb, 0, 0)),
            _const_spec(xa_wq.shape), _const_spec(xa_wo.shape),
            _const_spec(ln2_g.shape), _const_spec(ln2_b.shape),
            _const_spec(ffn_w_in.shape), _const_spec(ffn_w_out.shape),
            _const_spec(ln3_g.shape), _const_spec(ln3_b.shape),
        ],
        out_specs=row,
        out_shape=jax.ShapeDtypeStruct((M, D_MODEL), F32),
        compiler_params=_params("arbitrary"),
        name="xattn_ffn",
    )(x1, kv3, xa_wq, xa_wo, ln2_g, ln2_b, ffn_w_in, ffn_w_out, ln3_g, ln3_b)


def _rope_tables(T):
    inv = ROPE_THETA ** (-jnp.arange(0, ROT_DIM, 2, dtype=F32) / ROT_DIM)
    ang = jnp.arange(T, dtype=F32)[:, None] * inv[None, :]
    cos, sin = jnp.cos(ang), jnp.sin(ang)
    ones = jnp.ones((T, DA_HEAD_DIM - ROT_DIM), F32)
    zeros = jnp.zeros((T, DA_HEAD_DIM - ROT_DIM), F32)
    z8 = jnp.zeros((T, ROT_HALF), F32)
    cos64 = jnp.concatenate([cos, cos, ones], axis=1)
    sa64 = jnp.concatenate([z8, sin, zeros], axis=1)
    sb64 = jnp.concatenate([-sin, z8, zeros], axis=1)
    rep = LANES_V7X // DA_HEAD_DIM
    return (jnp.tile(cos64, (1, rep)), jnp.tile(sa64, (1, rep)), jnp.tile(sb64, (1, rep)))


def _trunk(x, mem, w):
    B, T, _ = x.shape
    M = B * T
    x2d = x.reshape(M, D_MODEL)
    cos_t, sa_t, sb_t = _rope_tables(T)
    qkv, rest = _proj(x2d, w["w_in"], cos_t, sa_t, sb_t, T)
    hf, hb = _lru(rest.reshape(B, T, 4 * D_MODEL), w["conv_w"], w["conv_b"], w["w_gates"],
                  w["b_gates"], w["lru_a"])
    attn = _attention(qkv.reshape(B, T, 3 * D_MODEL), w["lam4"], w["subln_g"])
    x1 = _merge(attn.reshape(M, D_MODEL), hf.reshape(M, D_MODEL), hb.reshape(M, D_MODEL), rest,
                x2d, w["p_attn"], w["p_lru"], w["w_mix_out"], w["ln1_g"], w["ln1_b"])
    kv = _memkv(mem.reshape(B * N_MEM, D_MODEL), w["xa_wkv"])
    y = _tail(x1, kv.reshape(B, N_MEM, 2 * D_MODEL), T, w["xa_wq"], w["xa_wo"], w["ln2_g"],
              w["ln2_b"], w["ffn_w_in"], w["ffn_w_out"], w["ln3_g"], w["ln3_b"])
    return y.reshape(B, T, D_MODEL)


def _prepare_weights(w_in, lambda_q1, lambda_k1, lambda_q2, lambda_k2, subln_g, conv_w, conv_b,
                     lru_wa, lru_ba, lru_wx, lru_bx, lru_a, p_attn, p_lru, w_mix_out, ln1_g, ln1_b,
                     xa_wq, xa_wkv, xa_wo, ln2_g, ln2_b, ffn_w_in, ffn_w_out, ln3_g, ln3_b):
    row = lambda v: v[0].reshape(1, -1).astype(F32)
    return {
        "w_in": w_in[0].astype(BF16),
        "lam4": jnp.stack([lambda_q1[0], lambda_k1[0], lambda_q2[0], lambda_k2[0]]).astype(F32),
        "subln_g": row(subln_g),
        "conv_w": conv_w[0].astype(F32),
        "conv_b": row(conv_b),
        "w_gates": jnp.concatenate([lru_wa[0], lru_wx[0]], axis=-1).astype(BF16),
        "b_gates": jnp.concatenate([lru_ba[0], lru_bx[0]], axis=-1).astype(F32),
        "lru_a": lru_a[0].astype(F32),
        "p_attn": p_attn[0].astype(BF16),
        "p_lru": p_lru[0].astype(BF16),
        "w_mix_out": w_mix_out[0].astype(BF16),
        "ln1_g": row(ln1_g), "ln1_b": row(ln1_b),
        "xa_wq": xa_wq[0].astype(BF16),
        "xa_wkv": xa_wkv[0].astype(BF16),
        "xa_wo": xa_wo[0].astype(BF16),
        "ln2_g": row(ln2_g), "ln2_b": row(ln2_b),
        "ffn_w_in": ffn_w_in[0].astype(BF16),
        "ffn_w_out": ffn_w_out[0].astype(BF16),
        "ln3_g": row(ln3_g), "ln3_b": row(ln3_b),
    }


def kernel(x_prompt, x_sample, mem_prompt, mem_sample, w_in, lambda_q1, lambda_k1, lambda_q2, lambda_k2, subln_g, conv_w, conv_b, lru_wa, lru_ba, lru_wx, lru_bx, lru_a, p_attn, p_lru, w_mix_out, ln1_g, ln1_b, xa_wq, xa_wkv, xa_wo, ln2_g, ln2_b, ffn_w_in, ffn_w_out, ln3_g, ln3_b):
    w = _prepare_weights(w_in, lambda_q1, lambda_k1, lambda_q2, lambda_k2, subln_g, conv_w, conv_b,
                         lru_wa, lru_ba, lru_wx, lru_bx, lru_a, p_attn, p_lru, w_mix_out, ln1_g,
                         ln1_b, xa_wq, xa_wkv, xa_wo, ln2_g, ln2_b, ffn_w_in, ffn_w_out, ln3_g, ln3_b)
    return (_trunk(x_prompt, mem_prompt, w), _trunk(x_sample, mem_sample, w))
```

```python
import functools
import math

import jax
import jax.numpy as jnp
from jax import lax
from jax.experimental import pallas as pl
from jax.experimental.pallas import tpu as pltpu

F32 = jnp.float32
BF16 = jnp.bfloat16

D_MODEL = 1024
N_MEM = 256
DA_HEADS = 8
DA_HEAD_DIM = 64
DA_PAIR = 2 * DA_HEAD_DIM
ROT_DIM = DA_HEAD_DIM // 4
ROT_HALF = ROT_DIM // 2
ROPE_THETA = 500000.0
SUBLN_EPS = 1e-5
LRU_BLOCKS = 8
LRU_BLOCK_DIM = D_MODEL // LRU_BLOCKS
CONV_WIDTH = 4
LRU_C = 8.0
N_SEG = 7
XA_HEADS = 4
XA_HEAD_DIM = D_MODEL // XA_HEADS
D_FF = ((8 * D_MODEL + 3 * 256 - 1) // (3 * 256)) * 256
DEPTH = 1
DEEPNORM_ALPHA = (2.0 * DEPTH) ** 0.25
LN_EPS = 1e-5
LAMBDA_INIT = 0.8 - 0.6 * math.exp(-0.3 * 0)

LANES_V7X = 128
SUBLANES_V7X = 8
VMEM_LIMIT_V7X = 56 * 1024 * 1024

PROJ_ROWS = 1024
LRU_CHUNK = 512
ATTN_TQ = 512
ATTN_TK = 512
ROW_TILE = 512
FFN_CHUNKS = ((0, 1024), (1024, 1024), (2048, D_FF - 2048))
NEG_BIG = -1e30


def _params(*sem):
    return pltpu.CompilerParams(dimension_semantics=sem, vmem_limit_bytes=VMEM_LIMIT_V7X)


def _const_spec(shape):
    nd = len(shape)
    return pl.BlockSpec(shape, lambda *_: (0,) * nd, pipeline_mode=pl.Buffered(1))


def _layer_norm(x, g, b):
    mu = jnp.mean(x, axis=-1, keepdims=True)
    xc = x - mu
    var = jnp.mean(xc * xc, axis=-1, keepdims=True)
    return xc * lax.rsqrt(var + LN_EPS) * g + b


def _proj_kernel(x_ref, w_ref, cos_ref, sa_ref, sb_ref, qkv_ref, rest_ref, xb_ref):
    j = pl.program_id(1)

    @pl.when(j == 0)
    def _():
        xb_ref[...] = x_ref[...].astype(BF16)

    def matmul():
        return jnp.dot(xb_ref[...], w_ref[...], preferred_element_type=F32)

    def rope_store(acc, scale):
        c, sa, sb = cos_ref[...], sa_ref[...], sb_ref[...]
        for cb in range(D_MODEL // LANES_V7X):
            sl = slice(cb * LANES_V7X, (cb + 1) * LANES_V7X)
            blk = acc[:, sl]
            r = (blk * c + pltpu.roll(blk, ROT_HALF, 1) * sa
                 + pltpu.roll(blk, LANES_V7X - ROT_HALF, 1) * sb)
            qkv_ref[:, sl] = (r * scale).astype(BF16)

    @pl.when(j == 0)
    def _():
        rope_store(matmul(), DA_HEAD_DIM ** -0.5)

    @pl.when(j == 1)
    def _():
        rope_store(matmul(), 1.0)

    @pl.when(j == 2)
    def _():
        qkv_ref[...] = matmul().astype(BF16)

    @pl.when(j >= 3)
    def _():
        rest_ref[...] = matmul()


def _proj(x2d, w_in, cos_t, sa_t, sb_t, T):
    M = x2d.shape[0]
    tm = min(PROJ_ROWS, T)
    tpb = T // tm
    return pl.pallas_call(
        _proj_kernel,
        grid=(M // tm, N_SEG),
        in_specs=[
            pl.BlockSpec((tm, D_MODEL), lambda i, j: (i, 0)),
            pl.BlockSpec((D_MODEL, D_MODEL), lambda i, j: (0, j)),
            pl.BlockSpec((tm, LANES_V7X), lambda i, j: (i % tpb, 0)),
            pl.BlockSpec((tm, LANES_V7X), lambda i, j: (i % tpb, 0)),
            pl.BlockSpec((tm, LANES_V7X), lambda i, j: (i % tpb, 0)),
        ],
        out_specs=[
            pl.BlockSpec((tm, D_MODEL), lambda i, j: (i, jnp.minimum(j, 2))),
            pl.BlockSpec((tm, D_MODEL), lambda i, j: (i, jnp.maximum(j - 3, 0))),
        ],
        out_shape=[
            jax.ShapeDtypeStruct((M, 3 * D_MODEL), BF16),
            jax.ShapeDtypeStruct((M, 4 * D_MODEL), F32),
        ],
        scratch_shapes=[pltpu.VMEM((tm, D_MODEL), BF16)],
        compiler_params=_params("arbitrary", "arbitrary"),
        name="proj",
    )(x2d, w_in, cos_t, sa_t, sb_t)


def _lru_kernel(xf_ref, xfp_ref, xfn_ref, xr_ref, xrp_ref, xrn_ref, cw_ref, cb_ref, w_ref,
                bias_ref, la_ref, hf_ref, hb_ref, af_ref, ab_ref, carry_ref, *, tc):
    c = pl.program_id(1)
    n_c = pl.num_programs(1)

    @pl.when(c == 0)
    def _():
        carry_ref[...] = jnp.zeros_like(carry_ref)

    rows = lax.broadcasted_iota(jnp.int32, (tc, LRU_BLOCK_DIM), 0)
    n_ext = tc + 2 * SUBLANES_V7X

    def prepare(x_ref, xp_ref, xn_ref, d, has_prev, has_next, start_mask, a_ref, u_ref):
        prev = jnp.where(has_prev, xp_ref[0], 0.0)
        nxt = jnp.where(has_next, xn_ref[0], 0.0)
        xall = jnp.concatenate([prev, x_ref[0], nxt], axis=0)
        conv = (cw_ref[0:1, :] * pltpu.roll(xall, 2, 0)
                + cw_ref[1:2, :] * pltpu.roll(xall, 1, 0)
                + cw_ref[2:3, :] * xall
                + cw_ref[3:4, :] * pltpu.roll(xall, n_ext - 1, 0))
        xc = conv[SUBLANES_V7X:SUBLANES_V7X + tc] + cb_ref[...]
        xcb = xc.astype(BF16)
        lam = la_ref[d:d + 1, :]
        decay = -LRU_C * (jnp.maximum(-lam, 0.0) + jnp.log1p(jnp.exp(-jnp.abs(lam))))
        for n in range(LRU_BLOCKS):
            sl = slice(n * LRU_BLOCK_DIM, (n + 1) * LRU_BLOCK_DIM)
            y = jnp.dot(xcb[:, sl], w_ref[d, n], preferred_element_type=F32)
            r = jax.nn.sigmoid(y[:, :LRU_BLOCK_DIM] + bias_ref[d:d + 1, sl])
            gate = jax.nn.sigmoid(
                y[:, LRU_BLOCK_DIM:]
                + bias_ref[d:d + 1, D_MODEL + n * LRU_BLOCK_DIM:D_MODEL + (n + 1) * LRU_BLOCK_DIM])
            a = jnp.exp(r * decay[:, sl])
            mult = jnp.where(start_mask, 1.0, jnp.sqrt(1.0 - a * a))
            a_ref[:, sl] = a
            u_ref[0, :, sl] = mult * gate * xc[:, sl]

    first = c == 0
    last = c == n_c - 1
    prepare(xf_ref, xfp_ref, xfn_ref, 0, jnp.logical_not(first), jnp.logical_not(last),
            jnp.logical_and(first, rows == 0), af_ref, hf_ref)
    prepare(xr_ref, xrp_ref, xrn_ref, 1, jnp.logical_not(last), jnp.logical_not(first),
            jnp.logical_and(first, rows == tc - 1), ab_ref, hb_ref)

    def step(t, carry):
        hf, hb = carry
        hf = af_ref[pl.ds(t, 1), :] * hf + hf_ref[0, pl.ds(t, 1), :]
        hf_ref[0, pl.ds(t, 1), :] = hf
        tb = tc - 1 - t
        hb = ab_ref[pl.ds(tb, 1), :] * hb + hb_ref[0, pl.ds(tb, 1), :]
        hb_ref[0, pl.ds(tb, 1), :] = hb
        return hf, hb

    hf, hb = lax.fori_loop(0, tc, step, (carry_ref[0:1, :], carry_ref[1:2, :]), unroll=8)
    carry_ref[0:1, :] = hf
    carry_ref[1:2, :] = hb


def _lru(rest3, conv_w, conv_b, w_gates, b_gates, lru_a):
    B, T, _ = rest3.shape
    tc = min(LRU_CHUNK, T)
    n_c = T // tc
    hb8 = tc // SUBLANES_V7X
    n8 = T // SUBLANES_V7X
    seg = 3 - 3

    def cur(f):
        return pl.BlockSpec((1, tc, D_MODEL), lambda b, c: (b, f(c, n_c), seg))

    def prev(f):
        return pl.BlockSpec((1, SUBLANES_V7X, D_MODEL),
                            lambda b, c: (b, jnp.maximum(f(c, n_c) * hb8 - 1, 0), seg))

    def nxt(f):
        return pl.BlockSpec((1, SUBLANES_V7X, D_MODEL),
                            lambda b, c: (b, jnp.minimum((f(c, n_c) + 1) * hb8, n8 - 1), seg))

    fwd = lambda c, n: c
    bwd = lambda c, n: n - 1 - c
    return pl.pallas_call(
        functools.partial(_lru_kernel, tc=tc),
        grid=(B, n_c),
        in_specs=[
            cur(fwd), prev(fwd), nxt(fwd), cur(bwd), prev(bwd), nxt(bwd),
            _const_spec(conv_w.shape), _const_spec(conv_b.shape), _const_spec(w_gates.shape),
            _const_spec(b_gates.shape), _const_spec(lru_a.shape),
        ],
        out_specs=[
            pl.BlockSpec((1, tc, D_MODEL), lambda b, c: (b, c, 0)),
            pl.BlockSpec((1, tc, D_MODEL), lambda b, c: (b, n_c - 1 - c, 0)),
        ],
        out_shape=[jax.ShapeDtypeStruct((B, T, D_MODEL), F32)] * 2,
        scratch_shapes=[
            pltpu.VMEM((tc, D_MODEL), F32),
            pltpu.VMEM((tc, D_MODEL), F32),
            pltpu.VMEM((2, D_MODEL), F32),
        ],
        compiler_params=_params("arbitrary", "arbitrary"),
        name="lru",
    )(rest3, rest3, rest3, rest3, rest3, rest3, conv_w, conv_b, w_gates, b_gates, lru_a)


ONES_ROWS = 16


def _attn_kernel(lam_ref, g_ref, q_ref, k_ref, v_ref, o_ref, vt_ref, st_ref, acc_ref, m_ref, *,
                 n_kv, tk):
    i = pl.program_id(2)

    @pl.when(i == 0)
    def _():
        def fill(c, _):
            off = pl.multiple_of(c * tk, tk)
            vt_ref[0:DA_PAIR, pl.ds(off, tk)] = v_ref[0, pl.ds(off, tk), :].astype(F32).T.astype(BF16)
            return 0
        lax.fori_loop(0, n_kv, fill, 0)
        vt_ref[DA_PAIR:DA_PAIR + ONES_ROWS, :] = jnp.ones((ONES_ROWS, vt_ref.shape[1]), BF16)

    tq = q_ref.shape[1]
    qt = q_ref[0].astype(F32).T
    row = lax.broadcasted_iota(jnp.int32, qt.shape, 0)
    qt2 = jnp.concatenate([jnp.where(row < DA_HEAD_DIM, qt, 0.0),
                           jnp.where(row >= DA_HEAD_DIM, qt, 0.0)], axis=1).astype(BF16)
    acc_ref[...] = jnp.zeros_like(acc_ref)
    m_ref[...] = jnp.full(m_ref.shape, NEG_BIG, F32)

    def scores(c, slot):
        off = pl.multiple_of(c * tk, tk)
        st_ref[slot] = jnp.dot(k_ref[0, pl.ds(off, tk), :], qt2, preferred_element_type=F32)

    def accumulate(c, slot):
        off = pl.multiple_of(c * tk, tk)
        st = st_ref[slot]
        m_old = m_ref[...]
        m_new = jnp.maximum(m_old, jnp.max(st, axis=0, keepdims=True))
        alpha = jnp.exp(m_old - m_new)
        pt = jnp.exp(st - m_new).astype(BF16)
        acc_ref[...] = alpha * acc_ref[...] + jnp.dot(vt_ref[:, pl.ds(off, tk)], pt,
                                                     preferred_element_type=F32)
        m_ref[...] = m_new

    scores(0, 0)

    def pair(j, _):
        scores(2 * j + 1, 1)
        accumulate(2 * j, 0)
        scores(2 * j + 2, 0)
        accumulate(2 * j + 1, 1)
        return 0

    lax.fori_loop(0, n_kv // 2 - 1, pair, 0)
    scores(n_kv - 1, 1)
    accumulate(n_kv - 2, 0)
    accumulate(n_kv - 1, 1)

    lam = (jnp.exp(jnp.sum(lam_ref[0:1, :] * lam_ref[1:2, :], axis=-1, keepdims=True))
           - jnp.exp(jnp.sum(lam_ref[2:3, :] * lam_ref[3:4, :], axis=-1, keepdims=True))
           + LAMBDA_INIT)
    ot = (acc_ref[0:DA_PAIR, 0:tq] / acc_ref[DA_PAIR:DA_PAIR + 1, 0:tq]
          - lam * (acc_ref[0:DA_PAIR, tq:2 * tq] / acc_ref[DA_PAIR:DA_PAIR + 1, tq:2 * tq]))
    o = ot.T
    o = o * lax.rsqrt(jnp.mean(o * o, axis=-1, keepdims=True) + SUBLN_EPS) * g_ref[...]
    o_ref[0] = (o * (1.0 - LAMBDA_INIT)).astype(BF16)


def _attention(qkv3, lam4, subln_g):
    B, T, _ = qkv3.shape
    tq = min(ATTN_TQ, T)
    tk = min(ATTN_TK, T)
    assert T % (2 * tk) == 0 and T % tq == 0, "key chunks are consumed in pairs"
    return pl.pallas_call(
        functools.partial(_attn_kernel, n_kv=T // tk, tk=tk),
        grid=(B, DA_HEADS, T // tq),
        in_specs=[
            _const_spec(lam4.shape), _const_spec(subln_g.shape),
            pl.BlockSpec((1, tq, DA_PAIR), lambda b, h, i: (b, i, h)),
            pl.BlockSpec((1, T, DA_PAIR), lambda b, h, i: (b, 0, DA_HEADS + h)),
            pl.BlockSpec((1, T, DA_PAIR), lambda b, h, i: (b, 0, 2 * DA_HEADS + h)),
        ],
        out_specs=pl.BlockSpec((1, tq, DA_PAIR), lambda b, h, i: (b, i, h)),
        out_shape=jax.ShapeDtypeStruct((B, T, D_MODEL), BF16),
        scratch_shapes=[
            pltpu.VMEM((DA_PAIR + ONES_ROWS, T), BF16),
            pltpu.VMEM((2, tk, 2 * tq), F32),
            pltpu.VMEM((DA_PAIR + ONES_ROWS, 2 * tq), F32),
            pltpu.VMEM((1, 2 * tq), F32),
        ],
        compiler_params=_params("arbitrary", "arbitrary", "arbitrary"),
        name="diff_attn",
    )(lam4, subln_g, qkv3, qkv3, qkv3)


def _merge_kernel(attn_ref, hf_ref, hb_ref, yr_ref, ga_ref, gl_ref, x_ref, pa_ref, plru_ref,
                  wo_ref, g_ref, b_ref, o_ref):
    a_proj = jnp.dot(attn_ref[...], pa_ref[...], preferred_element_type=F32)
    lru_out = ((hf_ref[...] + hb_ref[...]) * jax.nn.gelu(yr_ref[...])).astype(BF16)
    l_proj = jnp.dot(lru_out, plru_ref[...], preferred_element_type=F32)
    merged = jax.nn.sigmoid(ga_ref[...]) * a_proj + jax.nn.sigmoid(gl_ref[...]) * l_proj
    m = jnp.dot(merged.astype(BF16), wo_ref[...], preferred_element_type=F32)
    o_ref[...] = _layer_norm(DEEPNORM_ALPHA * x_ref[...] + m, g_ref[...], b_ref[...])


def _merge(attn2, hf2, hb2, rest2, x2d, p_attn, p_lru, w_mix_out, ln_g, ln_b):
    M = x2d.shape[0]
    tm = ROW_TILE
    row = lambda seg: pl.BlockSpec((tm, D_MODEL), lambda i: (i, seg))
    return pl.pallas_call(
        _merge_kernel,
        grid=(M // tm,),
        in_specs=[
            row(0), row(0), row(0), row(1), row(2), row(3), row(0),
            _const_spec(p_attn.shape), _const_spec(p_lru.shape), _const_spec(w_mix_out.shape),
            _const_spec(ln_g.shape), _const_spec(ln_b.shape),
        ],
        out_specs=row(0),
        out_shape=jax.ShapeDtypeStruct((M, D_MODEL), F32),
        compiler_params=_params("arbitrary"),
        name="merge_ln1",
    )(attn2, hf2, hb2, rest2, rest2, rest2, x2d, p_attn, p_lru, w_mix_out, ln_g, ln_b)


def _memkv_kernel(m_ref, w_ref, o_ref):
    o_ref[...] = jnp.dot(m_ref[...].astype(BF16), w_ref[...],
                         preferred_element_type=F32).astype(BF16)


def _memkv(mem2, xa_wkv):
    M = mem2.shape[0]
    tm = min(ROW_TILE, M)
    return pl.pallas_call(
        _memkv_kernel,
        grid=(M // tm, 2),
        in_specs=[
            pl.BlockSpec((tm, D_MODEL), lambda i, j: (i, 0)),
            pl.BlockSpec((D_MODEL, D_MODEL), lambda i, j: (0, j)),
        ],
        out_specs=pl.BlockSpec((tm, D_MODEL), lambda i, j: (i, j)),
        out_shape=jax.ShapeDtypeStruct((M, 2 * D_MODEL), BF16),
        compiler_params=_params("arbitrary", "arbitrary"),
        name="mem_kv",
    )(mem2, xa_wkv)


def _tail_kernel(x_ref, kv_ref, wq_ref, wo_ref, g2_ref, b2_ref, wi_ref, wf_ref, g3_ref, b3_ref,
                 o_ref):
    x1 = x_ref[...]
    q = (jnp.dot(x1.astype(BF16), wq_ref[...], preferred_element_type=F32)
         * (XA_HEAD_DIM ** -0.5)).astype(BF16)
    heads = []
    for h in range(XA_HEADS):
        sl = slice(h * XA_HEAD_DIM, (h + 1) * XA_HEAD_DIM)
        kh = kv_ref[0, :, sl]
        vh = kv_ref[0, :, D_MODEL + h * XA_HEAD_DIM:D_MODEL + (h + 1) * XA_HEAD_DIM]
        s = lax.dot_general(q[:, sl], kh, (((1,), (1,)), ((), ())), preferred_element_type=F32)
        e = jnp.exp(s - jnp.max(s, axis=-1, keepdims=True))
        p = e / jnp.sum(e, axis=-1, keepdims=True)
        heads.append(jnp.dot(p.astype(BF16), vh, preferred_element_type=F32).astype(BF16))
    xa = jnp.dot(jnp.concatenate(heads, axis=-1), wo_ref[...], preferred_element_type=F32)
    x2 = _layer_norm(DEEPNORM_ALPHA * x1 + xa, g2_ref[...], b2_ref[...])
    x2b = x2.astype(BF16)
    y = None
    for c0, cw in FFN_CHUNKS:
        g = jnp.dot(x2b, wi_ref[:, c0:c0 + cw], preferred_element_type=F32)
        u = jnp.dot(x2b, wi_ref[:, D_FF + c0:D_FF + c0 + cw], preferred_element_type=F32)
        hcb = (g * jax.nn.sigmoid(g) * u).astype(BF16)
        part = jnp.dot(hcb, wf_ref[c0:c0 + cw, :], preferred_element_type=F32)
        y = part if y is None else y + part
    o_ref[...] = _layer_norm(DEEPNORM_ALPHA * x2 + y, g3_ref[...], b3_ref[...])


def _tail(x1, kv3, T, xa_wq, xa_wo, ln2_g, ln2_b, ffn_w_in, ffn_w_out, ln3_g, ln3_b):
    M = x1.shape[0]
    tm = ROW_TILE
    tpb = T // tm
    row = pl.BlockSpec((tm, D_MODEL), lambda i: (i, 0))
    return pl.pallas_call(
        _tail_kernel,
        grid=(M // tm,),
        in_specs=[
            row,
            pl.BlockSpec((1, N_MEM, 2 * D_MODEL), lambda i: (i // tpb, 0, 0)),
            _const_spec(xa_wq.shape), _const_spec(xa_wo.shape),
            _const_spec(ln2_g.shape), _const_spec(ln2_b.shape),
            _const_spec(ffn_w_in.shape), _const_spec(ffn_w_out.shape),
            _const_spec(ln3_g.shape), _const_spec(ln3_b.shape),
        ],
        out_specs=row,
        out_shape=jax.ShapeDtypeStruct((M, D_MODEL), F32),
        compiler_params=_params("arbitrary"),
        name="xattn_ffn",
    )(x1, kv3, xa_wq, xa_wo, ln2_g, ln2_b, ffn_w_in, ffn_w_out, ln3_g, ln3_b)


def _rope_tables(T):
    inv = ROPE_THETA ** (-jnp.arange(0, ROT_DIM, 2, dtype=F32) / ROT_DIM)
    ang = jnp.arange(T, dtype=F32)[:, None] * inv[None, :]
    cos, sin = jnp.cos(ang), jnp.sin(ang)
    ones = jnp.ones((T, DA_HEAD_DIM - ROT_DIM), F32)
    zeros = jnp.zeros((T, DA_HEAD_DIM - ROT_DIM), F32)
    z8 = jnp.zeros((T, ROT_HALF), F32)
    cos64 = jnp.concatenate([cos, cos, ones], axis=1)
    sa64 = jnp.concatenate([z8, sin, zeros], axis=1)
    sb64 = jnp.concatenate([-sin, z8, zeros], axis=1)
    rep = LANES_V7X // DA_HEAD_DIM
    return (jnp.tile(cos64, (1, rep)), jnp.tile(sa64, (1, rep)), jnp.tile(sb64, (1, rep)))


def _trunk(x, mem, w):
    B, T, _ = x.shape
    M = B * T
    x2d = x.reshape(M, D_MODEL)
    cos_t, sa_t, sb_t = _rope_tables(T)
    qkv, rest = _proj(x2d, w["w_in"], cos_t, sa_t, sb_t, T)
    hf, hb = _lru(rest.reshape(B, T, 4 * D_MODEL), w["conv_w"], w["conv_b"], w["w_gates"],
                  w["b_gates"], w["lru_a"])
    attn = _attention(qkv.reshape(B, T, 3 * D_MODEL), w["lam4"], w["subln_g"])
    x1 = _merge(attn.reshape(M, D_MODEL), hf.reshape(M, D_MODEL), hb.reshape(M, D_MODEL), rest,
                x2d, w["p_attn"], w["p_lru"], w["w_mix_out"], w["ln1_g"], w["ln1_b"])
    kv = _memkv(mem.reshape(B * N_MEM, D_MODEL), w["xa_wkv"])
    y = _tail(x1, kv.reshape(B, N_MEM, 2 * D_MODEL), T, w["xa_wq"], w["xa_wo"], w["ln2_g"],
              w["ln2_b"], w["ffn_w_in"], w["ffn_w_out"], w["ln3_g"], w["ln3_b"])
    return y.reshape(B, T, D_MODEL)


def _prepare_weights(w_in, lambda_q1, lambda_k1, lambda_q2, lambda_k2, subln_g, conv_w, conv_b,
                     lru_wa, lru_ba, lru_wx, lru_bx, lru_a, p_attn, p_lru, w_mix_out, ln1_g, ln1_b,
                     xa_wq, xa_wkv, xa_wo, ln2_g, ln2_b, ffn_w_in, ffn_w_out, ln3_g, ln3_b):
    row = lambda v: v[0].reshape(1, -1).astype(F32)
    return {
        "w_in": w_in[0].astype(BF16),
        "lam4": jnp.stack([lambda_q1[0], lambda_k1[0], lambda_q2[0], lambda_k2[0]]).astype(F32),
        "subln_g": row(subln_g),
        "conv_w": conv_w[0].astype(F32),
        "conv_b": row(conv_b),
        "w_gates": jnp.concatenate([lru_wa[0], lru_wx[0]], axis=-1).astype(BF16),
        "b_gates": jnp.concatenate([lru_ba[0], lru_bx[0]], axis=-1).astype(F32),
        "lru_a": lru_a[0].astype(F32),
        "p_attn": p_attn[0].astype(BF16),
        "p_lru": p_lru[0].astype(BF16),
        "w_mix_out": w_mix_out[0].astype(BF16),
        "ln1_g": row(ln1_g), "ln1_b": row(ln1_b),
        "xa_wq": xa_wq[0].astype(BF16),
        "xa_wkv": xa_wkv[0].astype(BF16),
        "xa_wo": xa_wo[0].astype(BF16),
        "ln2_g": row(ln2_g), "ln2_b": row(ln2_b),
        "ffn_w_in": ffn_w_in[0].astype(BF16),
        "ffn_w_out": ffn_w_out[0].astype(BF16),
        "ln3_g": row(ln3_g), "ln3_b": row(ln3_b),
    }


def kernel(x_prompt, x_sample, mem_prompt, mem_sample, w_in, lambda_q1, lambda_k1, lambda_q2, lambda_k2, subln_g, conv_w, conv_b, lru_wa, lru_ba, lru_wx, lru_bx, lru_a, p_attn, p_lru, w_mix_out, ln1_g, ln1_b, xa_wq, xa_wkv, xa_wo, ln2_g, ln2_b, ffn_w_in, ffn_w_out, ln3_g, ln3_b):
    w = _prepare_weights(w_in, lambda_q1, lambda_k1, lambda_q2, lambda_k2, subln_g, conv_w, conv_b,
                         lru_wa, lru_ba, lru_wx, lru_bx, lru_a, p_attn, p_lru, w_mix_out, ln1_g,
                         ln1_b, xa_wq, xa_wkv, xa_wo, ln2_g, ln2_b, ffn_w_in, ffn_w_out, ln3_g, ln3_b)
    return (_trunk(x_prompt, mem_prompt, w), _trunk(x_sample, mem_sample, w))
```

```python
import functools
import math

import jax
import jax.numpy as jnp
from jax import lax
from jax.experimental import pallas as pl
from jax.experimental.pallas import tpu as pltpu

F32 = jnp.float32
BF16 = jnp.bfloat16

D_MODEL = 1024
N_MEM = 256
DA_HEADS = 8
DA_HEAD_DIM = 64
DA_PAIR = 2 * DA_HEAD_DIM
ROT_DIM = DA_HEAD_DIM // 4
ROT_HALF = ROT_DIM // 2
ROPE_THETA = 500000.0
SUBLN_EPS = 1e-5
LRU_BLOCKS = 8
LRU_BLOCK_DIM = D_MODEL // LRU_BLOCKS
CONV_WIDTH = 4
LRU_C = 8.0
N_SEG = 7
XA_HEADS = 4
XA_HEAD_DIM = D_MODEL // XA_HEADS
D_FF = ((8 * D_MODEL + 3 * 256 - 1) // (3 * 256)) * 256
DEPTH = 1
DEEPNORM_ALPHA = (2.0 * DEPTH) ** 0.25
LN_EPS = 1e-5
LAMBDA_INIT = 0.8 - 0.6 * math.exp(-0.3 * 0)

LANES_V7X = 128
SUBLANES_V7X = 8
VMEM_LIMIT_V7X = 56 * 1024 * 1024

PROJ_ROWS = 1024
LRU_CHUNK = 512
ATTN_TQ = 512
ATTN_TK = 512
ROW_TILE = 512
FFN_CHUNKS = ((0, 1024), (1024, 1024), (2048, D_FF - 2048))
NEG_BIG = -1e30
ONES_ROWS = 16


def _params(*sem):
    return pltpu.CompilerParams(dimension_semantics=sem, vmem_limit_bytes=VMEM_LIMIT_V7X)


def _const_spec(shape):
    nd = len(shape)
    return pl.BlockSpec(shape, lambda *_: (0,) * nd, pipeline_mode=pl.Buffered(1))


def _layer_norm(x, g, b):
    mu = jnp.mean(x, axis=-1, keepdims=True)
    xc = x - mu
    var = jnp.mean(xc * xc, axis=-1, keepdims=True)
    return xc * lax.rsqrt(var + LN_EPS) * g + b


def _proj_kernel(x_ref, w_ref, cos_ref, sa_ref, sb_ref, qk_ref, vt_ref, rest_ref, xb_ref):
    j = pl.program_id(1)

    @pl.when(j == 0)
    def _():
        xb_ref[...] = x_ref[...].astype(BF16)

    def matmul():
        return jnp.dot(xb_ref[...], w_ref[...], preferred_element_type=F32)

    def rope_store(acc, scale):
        c, sa, sb = cos_ref[...], sa_ref[...], sb_ref[...]
        for h in range(DA_HEADS):
            blk = acc[:, h * DA_PAIR:(h + 1) * DA_PAIR]
            r = (blk * c + pltpu.roll(blk, ROT_HALF, 1) * sa
                 + pltpu.roll(blk, LANES_V7X - ROT_HALF, 1) * sb)
            qk_ref[h] = (r * scale).astype(BF16)

    @pl.when(j == 0)
    def _():
        rope_store(matmul(), DA_HEAD_DIM ** -0.5)

    @pl.when(j == 1)
    def _():
        rope_store(matmul(), 1.0)

    @pl.when(j == 2)
    def _():
        acc = matmul()
        ones = jnp.ones((ONES_ROWS, acc.shape[0]), BF16)
        for h in range(DA_HEADS):
            vt_ref[0, h, 0:DA_PAIR, :] = acc[:, h * DA_PAIR:(h + 1) * DA_PAIR].T.astype(BF16)
            vt_ref[0, h, DA_PAIR:DA_PAIR + ONES_ROWS, :] = ones

    @pl.when(j >= 3)
    def _():
        rest_ref[...] = matmul()


def _proj(x2d, w_in, cos_t, sa_t, sb_t, T):
    M = x2d.shape[0]
    tm = min(PROJ_ROWS, T)
    tpb = T // tm
    return pl.pallas_call(
        _proj_kernel,
        grid=(M // tm, N_SEG),
        in_specs=[
            pl.BlockSpec((tm, D_MODEL), lambda i, j: (i, 0)),
            pl.BlockSpec((D_MODEL, D_MODEL), lambda i, j: (0, j)),
            pl.BlockSpec((tm, LANES_V7X), lambda i, j: (i % tpb, 0)),
            pl.BlockSpec((tm, LANES_V7X), lambda i, j: (i % tpb, 0)),
            pl.BlockSpec((tm, LANES_V7X), lambda i, j: (i % tpb, 0)),
        ],
        out_specs=[
            pl.BlockSpec((DA_HEADS, tm, DA_PAIR), lambda i, j: (jnp.minimum(j, 1), i, 0)),
            pl.BlockSpec((1, DA_HEADS, DA_PAIR + ONES_ROWS, tm),
                         lambda i, j: (i // tpb, 0, 0, i % tpb)),
            pl.BlockSpec((tm, D_MODEL), lambda i, j: (i, jnp.maximum(j - 3, 0))),
        ],
        out_shape=[
            jax.ShapeDtypeStruct((2 * DA_HEADS, M, DA_PAIR), BF16),
            jax.ShapeDtypeStruct((M // T, DA_HEADS, DA_PAIR + ONES_ROWS, T), BF16),
            jax.ShapeDtypeStruct((M, 4 * D_MODEL), F32),
        ],
        scratch_shapes=[pltpu.VMEM((tm, D_MODEL), BF16)],
        compiler_params=_params("arbitrary", "arbitrary"),
        name="proj",
    )(x2d, w_in, cos_t, sa_t, sb_t)


def _lru_kernel(xf_ref, xfp_ref, xfn_ref, xr_ref, xrp_ref, xrn_ref, cw_ref, cb_ref, w_ref,
                bias_ref, la_ref, hf_ref, hb_ref, af_ref, ab_ref, carry_ref, *, tc):
    c = pl.program_id(1)
    n_c = pl.num_programs(1)

    @pl.when(c == 0)
    def _():
        carry_ref[...] = jnp.zeros_like(carry_ref)

    rows = lax.broadcasted_iota(jnp.int32, (tc, LRU_BLOCK_DIM), 0)
    n_ext = tc + 2 * SUBLANES_V7X

    def prepare(x_ref, xp_ref, xn_ref, d, has_prev, has_next, start_mask, a_ref, u_ref):
        prev = jnp.where(has_prev, xp_ref[0], 0.0)
        nxt = jnp.where(has_next, xn_ref[0], 0.0)
        xall = jnp.concatenate([prev, x_ref[0], nxt], axis=0)
        conv = (cw_ref[0:1, :] * pltpu.roll(xall, 2, 0)
                + cw_ref[1:2, :] * pltpu.roll(xall, 1, 0)
                + cw_ref[2:3, :] * xall
                + cw_ref[3:4, :] * pltpu.roll(xall, n_ext - 1, 0))
        xc = conv[SUBLANES_V7X:SUBLANES_V7X + tc] + cb_ref[...]
        xcb = xc.astype(BF16)
        lam = la_ref[d:d + 1, :]
        decay = -LRU_C * (jnp.maximum(-lam, 0.0) + jnp.log1p(jnp.exp(-jnp.abs(lam))))
        for n in range(LRU_BLOCKS):
            sl = slice(n * LRU_BLOCK_DIM, (n + 1) * LRU_BLOCK_DIM)
            y = jnp.dot(xcb[:, sl], w_ref[d, n], preferred_element_type=F32)
            r = jax.nn.sigmoid(y[:, :LRU_BLOCK_DIM] + bias_ref[d:d + 1, sl])
            gate = jax.nn.sigmoid(
                y[:, LRU_BLOCK_DIM:]
                + bias_ref[d:d + 1, D_MODEL + n * LRU_BLOCK_DIM:D_MODEL + (n + 1) * LRU_BLOCK_DIM])
            a = jnp.exp(r * decay[:, sl])
            mult = jnp.where(start_mask, 1.0, jnp.sqrt(1.0 - a * a))
            a_ref[:, sl] = a
            u_ref[0, :, sl] = mult * gate * xc[:, sl]

    first = c == 0
    last = c == n_c - 1
    prepare(xf_ref, xfp_ref, xfn_ref, 0, jnp.logical_not(first), jnp.logical_not(last),
            jnp.logical_and(first, rows == 0), af_ref, hf_ref)
    prepare(xr_ref, xrp_ref, xrn_ref, 1, jnp.logical_not(last), jnp.logical_not(first),
            jnp.logical_and(first, rows == tc - 1), ab_ref, hb_ref)

    def step(t, carry):
        hf, hb = carry
        hf = af_ref[pl.ds(t, 1), :] * hf + hf_ref[0, pl.ds(t, 1), :]
        hf_ref[0, pl.ds(t, 1), :] = hf
        tb = tc - 1 - t
        hb = ab_ref[pl.ds(tb, 1), :] * hb + hb_ref[0, pl.ds(tb, 1), :]
        hb_ref[0, pl.ds(tb, 1), :] = hb
        return hf, hb

    hf, hb = lax.fori_loop(0, tc, step, (carry_ref[0:1, :], carry_ref[1:2, :]), unroll=8)
    carry_ref[0:1, :] = hf
    carry_ref[1:2, :] = hb


def _lru(rest3, conv_w, conv_b, w_gates, b_gates, lru_a):
    B, T, _ = rest3.shape
    tc = min(LRU_CHUNK, T)
    n_c = T // tc
    hb8 = tc // SUBLANES_V7X
    n8 = T // SUBLANES_V7X
    seg = 3 - 3

    def cur(f):
        return pl.BlockSpec((1, tc, D_MODEL), lambda b, c: (b, f(c, n_c), seg))

    def prev(f):
        return pl.BlockSpec((1, SUBLANES_V7X, D_MODEL),
                            lambda b, c: (b, jnp.maximum(f(c, n_c) * hb8 - 1, 0), seg))

    def nxt(f):
        return pl.BlockSpec((1, SUBLANES_V7X, D_MODEL),
                            lambda b, c: (b, jnp.minimum((f(c, n_c) + 1) * hb8, n8 - 1), seg))

    fwd = lambda c, n: c
    bwd = lambda c, n: n - 1 - c
    return pl.pallas_call(
        functools.partial(_lru_kernel, tc=tc),
        grid=(B, n_c),
        in_specs=[
            cur(fwd), prev(fwd), nxt(fwd), cur(bwd), prev(bwd), nxt(bwd),
            _const_spec(conv_w.shape), _const_spec(conv_b.shape), _const_spec(w_gates.shape),
            _const_spec(b_gates.shape), _const_spec(lru_a.shape),
        ],
        out_specs=[
            pl.BlockSpec((1, tc, D_MODEL), lambda b, c: (b, c, 0)),
            pl.BlockSpec((1, tc, D_MODEL), lambda b, c: (b, n_c - 1 - c, 0)),
        ],
        out_shape=[jax.ShapeDtypeStruct((B, T, D_MODEL), F32)] * 2,
        scratch_shapes=[
            pltpu.VMEM((tc, D_MODEL), F32),
            pltpu.VMEM((tc, D_MODEL), F32),
            pltpu.VMEM((2, D_MODEL), F32),
        ],
        compiler_params=_params("arbitrary", "arbitrary"),
        name="lru",
    )(rest3, rest3, rest3, rest3, rest3, rest3, conv_w, conv_b, w_gates, b_gates, lru_a)


def _attn_kernel(lam_ref, g_ref, q_ref, k_ref, vt_ref, o_ref, qt_ref, st_ref, acc_ref, m_ref, *,
                 n_q, n_kv, tq, tk):
    def prep_q(qi, slot):
        off = pl.multiple_of(qi * tq, tq)
        qt = q_ref[0, pl.ds(off, tq), :].astype(F32).T
        row = lax.broadcasted_iota(jnp.int32, qt.shape, 0)
        qt_ref[slot] = jnp.concatenate([jnp.where(row < DA_HEAD_DIM, qt, 0.0),
                                        jnp.where(row >= DA_HEAD_DIM, qt, 0.0)], axis=1).astype(BF16)

    def scores(qslot, c, slot):
        off = pl.multiple_of(c * tk, tk)
        st_ref[slot] = jnp.dot(k_ref[0, pl.ds(off, tk), :], qt_ref[qslot],
                               preferred_element_type=F32)

    def accumulate(c, slot):
        off = pl.multiple_of(c * tk, tk)
        st = st_ref[slot]
        m_old = m_ref[...]
        m_new = jnp.maximum(m_old, jnp.max(st, axis=0, keepdims=True))
        alpha = jnp.exp(m_old - m_new)
        pt = jnp.exp(st - m_new).astype(BF16)
        acc_ref[...] = alpha * acc_ref[...] + jnp.dot(vt_ref[0, 0, :, pl.ds(off, tk)], pt,
                                                     preferred_element_type=F32)
        m_ref[...] = m_new

    lam = (jnp.exp(jnp.sum(lam_ref[0:1, :] * lam_ref[1:2, :], axis=-1, keepdims=True))
           - jnp.exp(jnp.sum(lam_ref[2:3, :] * lam_ref[3:4, :], axis=-1, keepdims=True))
           + LAMBDA_INIT)

    def finalize(qi):
        ot = (acc_ref[0:DA_PAIR, 0:tq] / acc_ref[DA_PAIR:DA_PAIR + 1, 0:tq]
              - lam * (acc_ref[0:DA_PAIR, tq:2 * tq] / acc_ref[DA_PAIR:DA_PAIR + 1, tq:2 * tq]))
        o = ot.T
        o = o * lax.rsqrt(jnp.mean(o * o, axis=-1, keepdims=True) + SUBLN_EPS) * g_ref[...]
        off = pl.multiple_of(qi * tq, tq)
        o_ref[0, pl.ds(off, tq), :] = (o * (1.0 - LAMBDA_INIT)).astype(BF16)

    prep_q(0, 0)
    scores(0, 0, 0)

    def q_block(qi, _):
        qs = lax.rem(qi, 2)
        acc_ref[...] = jnp.zeros_like(acc_ref)
        m_ref[...] = jnp.full(m_ref.shape, NEG_BIG, F32)

        def pair(j, _):
            scores(qs, 2 * j + 1, 1)
            accumulate(2 * j, 0)
            scores(qs, 2 * j + 2, 0)
            accumulate(2 * j + 1, 1)
            return 0

        lax.fori_loop(0, n_kv // 2 - 1, pair, 0)
        scores(qs, n_kv - 1, 1)
        accumulate(n_kv - 2, 0)
        prep_q(jnp.minimum(qi + 1, n_q - 1), 1 - qs)
        scores(1 - qs, 0, 0)
        accumulate(n_kv - 1, 1)
        finalize(qi)
        return 0

    lax.fori_loop(0, n_q, q_block, 0)


def _attention(qk, vt, lam4, subln_g):
    B, _, _, T = vt.shape
    tq = min(ATTN_TQ, T)
    tk = min(ATTN_TK, T)
    assert T % (2 * tk) == 0 and T % tq == 0, "key chunks are consumed in pairs"
    return pl.pallas_call(
        functools.partial(_attn_kernel, n_q=T // tq, n_kv=T // tk, tq=tq, tk=tk),
        grid=(B, DA_HEADS),
        in_specs=[
            _const_spec(lam4.shape), _const_spec(subln_g.shape),
            pl.BlockSpec((1, T, DA_PAIR), lambda b, h: (h, b, 0)),
            pl.BlockSpec((1, T, DA_PAIR), lambda b, h: (DA_HEADS + h, b, 0)),
            pl.BlockSpec((1, 1, DA_PAIR + ONES_ROWS, T), lambda b, h: (b, h, 0, 0)),
        ],
        out_specs=pl.BlockSpec((1, T, DA_PAIR), lambda b, h: (b, 0, h)),
        out_shape=jax.ShapeDtypeStruct((B, T, D_MODEL), BF16),
        scratch_shapes=[
            pltpu.VMEM((2, DA_PAIR, 2 * tq), BF16),
            pltpu.VMEM((2, tk, 2 * tq), F32),
            pltpu.VMEM((DA_PAIR + ONES_ROWS, 2 * tq), F32),
            pltpu.VMEM((1, 2 * tq), F32),
        ],
        compiler_params=_params("arbitrary", "arbitrary"),
        name="diff_attn",
    )(lam4, subln_g, qk, qk, vt)


def _merge_kernel(attn_ref, hf_ref, hb_ref, yr_ref, ga_ref, gl_ref, x_ref, pa_ref, plru_ref,
                  wo_ref, g_ref, b_ref, o_ref):
    a_proj = jnp.dot(attn_ref[...], pa_ref[...], preferred_element_type=F32)
    lru_out = ((hf_ref[...] + hb_ref[...]) * jax.nn.gelu(yr_ref[...])).astype(BF16)
    l_proj = jnp.dot(lru_out, plru_ref[...], preferred_element_type=F32)
    merged = jax.nn.sigmoid(ga_ref[...]) * a_proj + jax.nn.sigmoid(gl_ref[...]) * l_proj
    m = jnp.dot(merged.astype(BF16), wo_ref[...], preferred_element_type=F32)
    o_ref[...] = _layer_norm(DEEPNORM_ALPHA * x_ref[...] + m, g_ref[...], b_ref[...])


def _merge(attn2, hf2, hb2, rest2, x2d, p_attn, p_lru, w_mix_out, ln_g, ln_b):
    M = x2d.shape[0]
    tm = ROW_TILE
    row = lambda seg: pl.BlockSpec((tm, D_MODEL), lambda i: (i, seg))
    return pl.pallas_call(
        _merge_kernel,
        grid=(M // tm,),
        in_specs=[
            row(0), row(0), row(0), row(1), row(2), row(3), row(0),
            _const_spec(p_attn.shape), _const_spec(p_lru.shape), _const_spec(w_mix_out.shape),
            _const_spec(ln_g.shape), _const_spec(ln_b.shape),
        ],
        out_specs=row(0),
        out_shape=jax.ShapeDtypeStruct((M, D_MODEL), F32),
        compiler_params=_params("arbitrary"),
        name="merge_ln1",
    )(attn2, hf2, hb2, rest2, rest2, rest2, x2d, p_attn, p_lru, w_mix_out, ln_g, ln_b)


def _memkv_kernel(m_ref, w_ref, o_ref):
    o_ref[...] = jnp.dot(m_ref[...].astype(BF16), w_ref[...],
                         preferred_element_type=F32).astype(BF16)


def _memkv(mem2, xa_wkv):
    M = mem2.shape[0]
    tm = min(ROW_TILE, M)
    return pl.pallas_call(
        _memkv_kernel,
        grid=(M // tm, 2),
        in_specs=[
            pl.BlockSpec((tm, D_MODEL), lambda i, j: (i, 0)),
            pl.BlockSpec((D_MODEL, D_MODEL), lambda i, j: (0, j)),
        ],
        out_specs=pl.BlockSpec((tm, D_MODEL), lambda i, j: (i, j)),
        out_shape=jax.ShapeDtypeStruct((M, 2 * D_MODEL), BF16),
        compiler_params=_params("arbitrary", "arbitrary"),
        name="mem_kv",
    )(mem2, xa_wkv)


def _tail_kernel(x_ref, kv_ref, wq_ref, wo_ref, g2_ref, b2_ref, wi_ref, wf_ref, g3_ref, b3_ref,
                 o_ref):
    x1 = x_ref[...]
    q = (jnp.dot(x1.astype(BF16), wq_ref[...], preferred_element_type=F32)
         * (XA_HEAD_DIM ** -0.5)).astype(BF16)
    heads = []
    for h in range(XA_HEADS):
        sl = slice(h * XA_HEAD_DIM, (h + 1) * XA_HEAD_DIM)
        kh = kv_ref[0, :, sl]
        vh = kv_ref[0, :, D_MODEL + h * XA_HEAD_DIM:D_MODEL + (h + 1) * XA_HEAD_DIM]
        s = lax.dot_general(q[:, sl], kh, (((1,), (1,)), ((), ())), preferred_element_type=F32)
        e = jnp.exp(s - jnp.max(s, axis=-1, keepdims=True))
        p = e / jnp.sum(e, axis=-1, keepdims=True)
        heads.append(jnp.dot(p.astype(BF16), vh, preferred_element_type=F32).astype(BF16))
    xa = jnp.dot(jnp.concatenate(heads, axis=-1), wo_ref[...], preferred_element_type=F32)
    x2 = _layer_norm(DEEPNORM_ALPHA * x1 + xa, g2_ref[...], b2_ref[...])
    x2b = x2.astype(BF16)
    y = None
    for c0, cw in FFN_CHUNKS:
        g = jnp.dot(x2b, wi_ref[:, c0:c0 + cw], preferred_element_type=F32)
        u = jnp.dot(x2b, wi_ref[:, D_FF + c0:D_FF + c0 + cw], preferred_element_type=F32)
        hcb = (g * jax.nn.sigmoid(g) * u).astype(BF16)
        part = jnp.dot(hcb, wf_ref[c0:c0 + cw, :], preferred_element_type=F32)
        y = part if y is None else y + part
    o_ref[...] = _layer_norm(DEEPNORM_ALPHA * x2 + y, g3_ref[...], b3_ref[...])


def _tail(x1, kv3, T, xa_wq, xa_wo, ln2_g, ln2_b, ffn_w_in, ffn_w_out, ln3_g, ln3_b):
    M = x1.shape[0]
    tm = ROW_TILE
    tpb = T // tm
    row = pl.BlockSpec((tm, D_MODEL), lambda i: (i, 0))
    return pl.pallas_call(
        _tail_kernel,
        grid=(M // tm,),
        in_specs=[
            row,
            pl.BlockSpec((1, N_MEM, 2 * D_MODEL), lambda i: (i // tpb, 0, 0)),
            _const_spec(xa_wq.shape), _const_spec(xa_wo.shape),
            _const_spec(ln2_g.shape), _const_spec(ln2_b.shape),
            _const_spec(ffn_w_in.shape), _const_spec(ffn_w_out.shape),
            _const_spec(ln3_g.shape), _const_spec(ln3_b.shape),
        ],
        out_specs=row,
        out_shape=jax.ShapeDtypeStruct((M, D_MODEL), F32),
        compiler_params=_params("arbitrary"),
        name="xattn_ffn",
    )(x1, kv3, xa_wq, xa_wo, ln2_g, ln2_b, ffn_w_in, ffn_w_out, ln3_g, ln3_b)


def _rope_tables(T):
    inv = ROPE_THETA ** (-jnp.arange(0, ROT_DIM, 2, dtype=F32) / ROT_DIM)
    ang = jnp.arange(T, dtype=F32)[:, None] * inv[None, :]
    cos, sin = jnp.cos(ang), jnp.sin(ang)
    ones = jnp.ones((T, DA_HEAD_DIM - ROT_DIM), F32)
    zeros = jnp.zeros((T, DA_HEAD_DIM - ROT_DIM), F32)
    z8 = jnp.zeros((T, ROT_HALF), F32)
    cos64 = jnp.concatenate([cos, cos, ones], axis=1)
    sa64 = jnp.concatenate([z8, sin, zeros], axis=1)
    sb64 = jnp.concatenate([-sin, z8, zeros], axis=1)
    rep = LANES_V7X // DA_HEAD_DIM
    return (jnp.tile(cos64, (1, rep)), jnp.tile(sa64, (1, rep)), jnp.tile(sb64, (1, rep)))


def _trunk(x, mem, w):
    B, T, _ = x.shape
    M = B * T
    x2d = x.reshape(M, D_MODEL)
    cos_t, sa_t, sb_t = _rope_tables(T)
    qk, vt, rest = _proj(x2d, w["w_in"], cos_t, sa_t, sb_t, T)
    hf, hb = _lru(rest.reshape(B, T, 4 * D_MODEL), w["conv_w"], w["conv_b"], w["w_gates"],
                  w["b_gates"], w["lru_a"])
    attn = _attention(qk, vt, w["lam4"], w["subln_g"])
    x1 = _merge(attn.reshape(M, D_MODEL), hf.reshape(M, D_MODEL), hb.reshape(M, D_MODEL), rest,
                x2d, w["p_attn"], w["p_lru"], w["w_mix_out"], w["ln1_g"], w["ln1_b"])
    kv = _memkv(mem.reshape(B * N_MEM, D_MODEL), w["xa_wkv"])
    y = _tail(x1, kv.reshape(B, N_MEM, 2 * D_MODEL), T, w["xa_wq"], w["xa_wo"], w["ln2_g"],
              w["ln2_b"], w["ffn_w_in"], w["ffn_w_out"], w["ln3_g"], w["ln3_b"])
    return y.reshape(B, T, D_MODEL)


def _prepare_weights(w_in, lambda_q1, lambda_k1, lambda_q2, lambda_k2, subln_g, conv_w, conv_b,
                     lru_wa, lru_ba, lru_wx, lru_bx, lru_a, p_attn, p_lru, w_mix_out, ln1_g, ln1_b,
                     xa_wq, xa_wkv, xa_wo, ln2_g, ln2_b, ffn_w_in, ffn_w_out, ln3_g, ln3_b):
    row = lambda v: v[0].reshape(1, -1).astype(F32)
    return {
        "w_in": w_in[0].astype(BF16),
        "lam4": jnp.stack([lambda_q1[0], lambda_k1[0], lambda_q2[0], lambda_k2[0]]).astype(F32),
        "subln_g": row(subln_g),
        "conv_w": conv_w[0].astype(F32),
        "conv_b": row(conv_b),
        "w_gates": jnp.concatenate([lru_wa[0], lru_wx[0]], axis=-1).astype(BF16),
        "b_gates": jnp.concatenate([lru_ba[0], lru_bx[0]], axis=-1).astype(F32),
        "lru_a": lru_a[0].astype(F32),
        "p_attn": p_attn[0].astype(BF16),
        "p_lru": p_lru[0].astype(BF16),
        "w_mix_out": w_mix_out[0].astype(BF16),
        "ln1_g": row(ln1_g), "ln1_b": row(ln1_b),
        "xa_wq": xa_wq[0].astype(BF16),
        "xa_wkv": xa_wkv[0].astype(BF16),
        "xa_wo": xa_wo[0].astype(BF16),
        "ln2_g": row(ln2_g), "ln2_b": row(ln2_b),
        "ffn_w_in": ffn_w_in[0].astype(BF16),
        "ffn_w_out": ffn_w_out[0].astype(BF16),
        "ln3_g": row(ln3_g), "ln3_b": row(ln3_b),
    }


def kernel(x_prompt, x_sample, mem_prompt, mem_sample, w_in, lambda_q1, lambda_k1, lambda_q2, lambda_k2, subln_g, conv_w, conv_b, lru_wa, lru_ba, lru_wx, lru_bx, lru_a, p_attn, p_lru, w_mix_out, ln1_g, ln1_b, xa_wq, xa_wkv, xa_wo, ln2_g, ln2_b, ffn_w_in, ffn_w_out, ln3_g, ln3_b):
    w = _prepare_weights(w_in, lambda_q1, lambda_k1, lambda_q2, lambda_k2, subln_g, conv_w, conv_b,
                         lru_wa, lru_ba, lru_wx, lru_bx, lru_a, p_attn, p_lru, w_mix_out, ln1_g,
                         ln1_b, xa_wq, xa_wkv, xa_wo, ln2_g, ln2_b, ffn_w_in, ffn_w_out, ln3_g, ln3_b)
    return (_trunk(x_prompt, mem_prompt, w), _trunk(x_sample, mem_sample, w))
```

```python
import functools
import math

import jax
import jax.numpy as jnp
from jax import lax
from jax.experimental import pallas as pl
from jax.experimental.pallas import tpu as pltpu

F32 = jnp.float32
BF16 = jnp.bfloat16

D_MODEL = 1024
N_MEM = 256
DA_HEADS = 8
DA_HEAD_DIM = 64
DA_PAIR = 2 * DA_HEAD_DIM
ROT_DIM = DA_HEAD_DIM // 4
ROT_HALF = ROT_DIM // 2
ROPE_THETA = 500000.0
SUBLN_EPS = 1e-5
LRU_BLOCKS = 8
LRU_BLOCK_DIM = D_MODEL // LRU_BLOCKS
CONV_WIDTH = 4
LRU_C = 8.0
N_SEG = 7
XA_HEADS = 4
XA_HEAD_DIM = D_MODEL // XA_HEADS
D_FF = ((8 * D_MODEL + 3 * 256 - 1) // (3 * 256)) * 256
DEPTH = 1
DEEPNORM_ALPHA = (2.0 * DEPTH) ** 0.25
LN_EPS = 1e-5
LAMBDA_INIT = 0.8 - 0.6 * math.exp(-0.3 * 0)

LANES_V7X = 128
SUBLANES_V7X = 8
VMEM_LIMIT_V7X = 56 * 1024 * 1024

PROJ_ROWS = 512
LRU_CHUNK = 512
ATTN_TQ = 512
ATTN_TK = 1024
ROW_TILE = 512
FFN_CHUNKS = ((0, 1024), (1024, 1024), (2048, D_FF - 2048))
NEG_BIG = -1e30
ONES_ROWS = 16


def _params(*sem):
    return pltpu.CompilerParams(dimension_semantics=sem, vmem_limit_bytes=VMEM_LIMIT_V7X)


def _const_spec(shape):
    nd = len(shape)
    return pl.BlockSpec(shape, lambda *_: (0,) * nd, pipeline_mode=pl.Buffered(1))


def _layer_norm(x, g, b):
    mu = jnp.mean(x, axis=-1, keepdims=True)
    xc = x - mu
    var = jnp.mean(xc * xc, axis=-1, keepdims=True)
    return xc * lax.rsqrt(var + LN_EPS) * g + b


def _proj_kernel(x_ref, w_ref, cos_ref, sa_ref, sb_ref, qk_ref, vt_ref, rest_ref):
    xb = x_ref[...].astype(BF16)

    def matmul(seg):
        return jnp.dot(xb, w_ref[:, seg * D_MODEL:(seg + 1) * D_MODEL], preferred_element_type=F32)

    def rope_store(acc, scale, base):
        c, sa, sb = cos_ref[...], sa_ref[...], sb_ref[...]
        for h in range(DA_HEADS):
            blk = acc[:, h * DA_PAIR:(h + 1) * DA_PAIR]
            r = (blk * c + pltpu.roll(blk, ROT_HALF, 1) * sa
                 + pltpu.roll(blk, LANES_V7X - ROT_HALF, 1) * sb)
            qk_ref[base + h] = (r * scale).astype(BF16)

    rope_store(matmul(0), DA_HEAD_DIM ** -0.5 * math.log2(math.e), 0)
    rope_store(matmul(1), 1.0, DA_HEADS)
    acc = matmul(2)
    ones = jnp.ones((ONES_ROWS, acc.shape[0]), BF16)
    for h in range(DA_HEADS):
        vt_ref[0, h, 0:DA_PAIR, :] = acc[:, h * DA_PAIR:(h + 1) * DA_PAIR].T.astype(BF16)
        vt_ref[0, h, DA_PAIR:DA_PAIR + ONES_ROWS, :] = ones
    for seg in range(3, N_SEG):
        rest_ref[:, (seg - 3) * D_MODEL:(seg - 2) * D_MODEL] = matmul(seg)


def _proj(x2d, w_in, cos_t, sa_t, sb_t, T):
    M = x2d.shape[0]
    tm = min(PROJ_ROWS, T)
    tpb = T // tm
    return pl.pallas_call(
        _proj_kernel,
        grid=(M // tm,),
        in_specs=[
            pl.BlockSpec((tm, D_MODEL), lambda i: (i, 0)),
            _const_spec(w_in.shape),
            pl.BlockSpec((tm, LANES_V7X), lambda i: (i % tpb, 0)),
            pl.BlockSpec((tm, LANES_V7X), lambda i: (i % tpb, 0)),
            pl.BlockSpec((tm, LANES_V7X), lambda i: (i % tpb, 0)),
        ],
        out_specs=[
            pl.BlockSpec((2 * DA_HEADS, tm, DA_PAIR), lambda i: (0, i, 0)),
            pl.BlockSpec((1, DA_HEADS, DA_PAIR + ONES_ROWS, tm),
                         lambda i: (i // tpb, 0, 0, i % tpb)),
            pl.BlockSpec((tm, 4 * D_MODEL), lambda i: (i, 0)),
        ],
        out_shape=[
            jax.ShapeDtypeStruct((2 * DA_HEADS, M, DA_PAIR), BF16),
            jax.ShapeDtypeStruct((M // T, DA_HEADS, DA_PAIR + ONES_ROWS, T), BF16),
            jax.ShapeDtypeStruct((M, 4 * D_MODEL), F32),
        ],
        compiler_params=_params("arbitrary"),
        name="proj",
    )(x2d, w_in, cos_t, sa_t, sb_t)


def _lru_kernel(xf_ref, xfp_ref, xfn_ref, xr_ref, xrp_ref, xrn_ref, cw_ref, cb_ref, w_ref,
                bias_ref, la_ref, hf_ref, hb_ref, af_ref, ab_ref, carry_ref, *, tc):
    c = pl.program_id(1)
    n_c = pl.num_programs(1)

    @pl.when(c == 0)
    def _():
        carry_ref[...] = jnp.zeros_like(carry_ref)

    n_ext = tc + 2 * SUBLANES_V7X

    def prepare(x_ref, xp_ref, xn_ref, d, has_prev, has_next, at_start, start_row, a_ref, u_ref):
        prev = jnp.where(has_prev, xp_ref[0], 0.0)
        nxt = jnp.where(has_next, xn_ref[0], 0.0)
        xall = jnp.concatenate([prev, x_ref[0], nxt], axis=0)
        conv = (cw_ref[0:1, :] * pltpu.roll(xall, 2, 0)
                + cw_ref[1:2, :] * pltpu.roll(xall, 1, 0)
                + cw_ref[2:3, :] * xall
                + cw_ref[3:4, :] * pltpu.roll(xall, n_ext - 1, 0))
        xc = conv[SUBLANES_V7X:SUBLANES_V7X + tc] + cb_ref[...]
        xcb = xc.astype(BF16)
        lam = la_ref[d:d + 1, :]
        decay = -LRU_C * (jnp.maximum(-lam, 0.0) + jnp.log1p(jnp.exp(-jnp.abs(lam))))
        for n in range(LRU_BLOCKS):
            sl = slice(n * LRU_BLOCK_DIM, (n + 1) * LRU_BLOCK_DIM)
            y = jnp.dot(xcb[:, sl], w_ref[d, n], preferred_element_type=F32)
            r = jax.nn.sigmoid(y[:, :LRU_BLOCK_DIM] + bias_ref[d:d + 1, sl])
            gate = jax.nn.sigmoid(
                y[:, LRU_BLOCK_DIM:]
                + bias_ref[d:d + 1, D_MODEL + n * LRU_BLOCK_DIM:D_MODEL + (n + 1) * LRU_BLOCK_DIM])
            a = jnp.exp(r * decay[:, sl])
            m2 = 1.0 - a * a
            mult = jnp.where(m2 > 0.0, m2 * lax.rsqrt(m2), 0.0)
            gx = gate * xc[:, sl]
            a_ref[:, sl] = a
            u_ref[0, :, sl] = mult * gx
            u_ref[0, start_row:start_row + 1, sl] = jnp.where(
                at_start, gx[start_row:start_row + 1], (mult * gx)[start_row:start_row + 1])

    first = c == 0
    last = c == n_c - 1
    prepare(xf_ref, xfp_ref, xfn_ref, 0, jnp.logical_not(first), jnp.logical_not(last),
            first, 0, af_ref, hf_ref)
    prepare(xr_ref, xrp_ref, xrn_ref, 1, jnp.logical_not(last), jnp.logical_not(first),
            first, tc - 1, ab_ref, hb_ref)

    def step(t, carry):
        hf, hb = carry
        hf = af_ref[pl.ds(t, 1), :] * hf + hf_ref[0, pl.ds(t, 1), :]
        hf_ref[0, pl.ds(t, 1), :] = hf
        tb = tc - 1 - t
        hb = ab_ref[pl.ds(tb, 1), :] * hb + hb_ref[0, pl.ds(tb, 1), :]
        hb_ref[0, pl.ds(tb, 1), :] = hb
        return hf, hb

    hf, hb = lax.fori_loop(0, tc, step, (carry_ref[0:1, :], carry_ref[1:2, :]), unroll=8)
    carry_ref[0:1, :] = hf
    carry_ref[1:2, :] = hb


def _lru(rest3, conv_w, conv_b, w_gates, b_gates, lru_a):
    B, T, _ = rest3.shape
    tc = min(LRU_CHUNK, T)
    n_c = T // tc
    hb8 = tc // SUBLANES_V7X
    n8 = T // SUBLANES_V7X
    seg = 3 - 3

    def cur(f):
        return pl.BlockSpec((1, tc, D_MODEL), lambda b, c: (b, f(c, n_c), seg))

    def prev(f):
        return pl.BlockSpec((1, SUBLANES_V7X, D_MODEL),
                            lambda b, c: (b, jnp.maximum(f(c, n_c) * hb8 - 1, 0), seg))

    def nxt(f):
        return pl.BlockSpec((1, SUBLANES_V7X, D_MODEL),
                            lambda b, c: (b, jnp.minimum((f(c, n_c) + 1) * hb8, n8 - 1), seg))

    fwd = lambda c, n: c
    bwd = lambda c, n: n - 1 - c
    return pl.pallas_call(
        functools.partial(_lru_kernel, tc=tc),
        grid=(B, n_c),
        in_specs=[
            cur(fwd), prev(fwd), nxt(fwd), cur(bwd), prev(bwd), nxt(bwd),
            _const_spec(conv_w.shape), _const_spec(conv_b.shape), _const_spec(w_gates.shape),
            _const_spec(b_gates.shape), _const_spec(lru_a.shape),
        ],
        out_specs=[
            pl.BlockSpec((1, tc, D_MODEL), lambda b, c: (b, c, 0)),
            pl.BlockSpec((1, tc, D_MODEL), lambda b, c: (b, n_c - 1 - c, 0)),
        ],
        out_shape=[jax.ShapeDtypeStruct((B, T, D_MODEL), F32)] * 2,
        scratch_shapes=[
            pltpu.VMEM((tc, D_MODEL), F32),
            pltpu.VMEM((tc, D_MODEL), F32),
            pltpu.VMEM((2, D_MODEL), F32),
        ],
        compiler_params=_params("arbitrary", "arbitrary"),
        name="lru",
    )(rest3, rest3, rest3, rest3, rest3, rest3, conv_w, conv_b, w_gates, b_gates, lru_a)


def _attn_kernel(lam_ref, g_ref, q_ref, k_ref, vt_ref, o_ref, qt_ref, st_ref, acc_ref, m_ref, *,
                 n_q, n_kv, tq, tk):
    def prep_q(qi, slot):
        off = pl.multiple_of(qi * tq, tq)
        qt = q_ref[0, pl.ds(off, tq), :].astype(F32).T
        row = lax.broadcasted_iota(jnp.int32, qt.shape, 0)
        qt_ref[slot] = jnp.concatenate([jnp.where(row < DA_HEAD_DIM, qt, 0.0),
                                        jnp.where(row >= DA_HEAD_DIM, qt, 0.0)], axis=1).astype(BF16)

    def scores(qslot, c, slot):
        off = pl.multiple_of(c * tk, tk)
        st_ref[slot] = jnp.dot(k_ref[0, pl.ds(off, tk), :], qt_ref[qslot],
                               preferred_element_type=F32)

    def accumulate(c, slot):
        off = pl.multiple_of(c * tk, tk)
        st = st_ref[slot]
        m_old = m_ref[...]
        m_new = jnp.maximum(m_old, jnp.max(st, axis=0, keepdims=True))
        alpha = jnp.exp2(m_old - m_new)
        pt = jnp.exp2(st - m_new).astype(BF16)
        acc_ref[...] = alpha * acc_ref[...] + jnp.dot(vt_ref[0, 0, :, pl.ds(off, tk)], pt,
                                                     preferred_element_type=F32)
        m_ref[...] = m_new

    lam = (jnp.exp(jnp.sum(lam_ref[0:1, :] * lam_ref[1:2, :], axis=-1, keepdims=True))
           - jnp.exp(jnp.sum(lam_ref[2:3, :] * lam_ref[3:4, :], axis=-1, keepdims=True))
           + LAMBDA_INIT)

    def finalize(qi):
        ot = (acc_ref[0:DA_PAIR, 0:tq] / acc_ref[DA_PAIR:DA_PAIR + 1, 0:tq]
              - lam * (acc_ref[0:DA_PAIR, tq:2 * tq] / acc_ref[DA_PAIR:DA_PAIR + 1, tq:2 * tq]))
        o = ot.T
        o = o * lax.rsqrt(jnp.mean(o * o, axis=-1, keepdims=True) + SUBLN_EPS) * g_ref[...]
        off = pl.multiple_of(qi * tq, tq)
        o_ref[0, pl.ds(off, tq), :] = (o * (1.0 - LAMBDA_INIT)).astype(BF16)

    prep_q(0, 0)
    scores(0, 0, 0)

    def q_block(qi, _):
        qs = lax.rem(qi, 2)
        acc_ref[...] = jnp.zeros_like(acc_ref)
        m_ref[...] = jnp.full(m_ref.shape, NEG_BIG, F32)

        def pair(j, _):
            scores(qs, 2 * j + 1, 1)
            accumulate(2 * j, 0)
            scores(qs, 2 * j + 2, 0)
            accumulate(2 * j + 1, 1)
            return 0

        lax.fori_loop(0, n_kv // 2 - 1, pair, 0)
        scores(qs, n_kv - 1, 1)
        accumulate(n_kv - 2, 0)
        prep_q(jnp.minimum(qi + 1, n_q - 1), 1 - qs)
        scores(1 - qs, 0, 0)
        accumulate(n_kv - 1, 1)
        finalize(qi)
        return 0

    lax.fori_loop(0, n_q, q_block, 0)


def _attention(qk, vt, lam4, subln_g):
    B, _, _, T = vt.shape
    tq = min(ATTN_TQ, T)
    tk = min(ATTN_TK, T)
    assert T % (2 * tk) == 0 and T % tq == 0, "key chunks are consumed in pairs"
    return pl.pallas_call(
        functools.partial(_attn_kernel, n_q=T // tq, n_kv=T // tk, tq=tq, tk=tk),
        grid=(B, DA_HEADS),
        in_specs=[
            _const_spec(lam4.shape), _const_spec(subln_g.shape),
            pl.BlockSpec((1, T, DA_PAIR), lambda b, h: (h, b, 0)),
            pl.BlockSpec((1, T, DA_PAIR), lambda b, h: (DA_HEADS + h, b, 0)),
            pl.BlockSpec((1, 1, DA_PAIR + ONES_ROWS, T), lambda b, h: (b, h, 0, 0)),
        ],
        out_specs=pl.BlockSpec((1, T, DA_PAIR), lambda b, h: (b, 0, h)),
        out_shape=jax.ShapeDtypeStruct((B, T, D_MODEL), BF16),
        scratch_shapes=[
            pltpu.VMEM((2, DA_PAIR, 2 * tq), BF16),
            pltpu.VMEM((2, tk, 2 * tq), F32),
            pltpu.VMEM((DA_PAIR + ONES_ROWS, 2 * tq), F32),
            pltpu.VMEM((1, 2 * tq), F32),
        ],
        compiler_params=_params("arbitrary", "arbitrary"),
        name="diff_attn",
    )(lam4, subln_g, qk, qk, vt)


def _merge_kernel(attn_ref, hf_ref, hb_ref, yr_ref, ga_ref, gl_ref, x_ref, pa_ref, plru_ref,
                  wo_ref, g_ref, b_ref, o_ref):
    a_proj = jnp.dot(attn_ref[...], pa_ref[...], preferred_element_type=F32)
    lru_out = ((hf_ref[...] + hb_ref[...]) * jax.nn.gelu(yr_ref[...])).astype(BF16)
    l_proj = jnp.dot(lru_out, plru_ref[...], preferred_element_type=F32)
    merged = jax.nn.sigmoid(ga_ref[...]) * a_proj + jax.nn.sigmoid(gl_ref[...]) * l_proj
    m = jnp.dot(merged.astype(BF16), wo_ref[...], preferred_element_type=F32)
    o_ref[...] = _layer_norm(DEEPNORM_ALPHA * x_ref[...] + m, g_ref[...], b_ref[...])


def _merge(attn2, hf2, hb2, rest2, x2d, p_attn, p_lru, w_mix_out, ln_g, ln_b):
    M = x2d.shape[0]
    tm = ROW_TILE
    row = lambda seg: pl.BlockSpec((tm, D_MODEL), lambda i: (i, seg))
    return pl.pallas_call(
        _merge_kernel,
        grid=(M // tm,),
        in_specs=[
            row(0), row(0), row(0), row(1), row(2), row(3), row(0),
            _const_spec(p_attn.shape), _const_spec(p_lru.shape), _const_spec(w_mix_out.shape),
            _const_spec(ln_g.shape), _const_spec(ln_b.shape),
        ],
        out_specs=row(0),
        out_shape=jax.ShapeDtypeStruct((M, D_MODEL), F32),
        compiler_params=_params("arbitrary"),
        name="merge_ln1",
    )(attn2, hf2, hb2, rest2, rest2, rest2, x2d, p_attn, p_lru, w_mix_out, ln_g, ln_b)


def _memkv_kernel(m_ref, w_ref, o_ref):
    o_ref[...] = jnp.dot(m_ref[...].astype(BF16), w_ref[...],
                         preferred_element_type=F32).astype(BF16)


def _memkv(mem2, xa_wkv):
    M = mem2.shape[0]
    tm = min(ROW_TILE, M)
    return pl.pallas_call(
        _memkv_kernel,
        grid=(M // tm, 2),
        in_specs=[
            pl.BlockSpec((tm, D_MODEL), lambda i, j: (i, 0)),
            pl.BlockSpec((D_MODEL, D_MODEL), lambda i, j: (0, j)),
        ],
        out_specs=pl.BlockSpec((tm, D_MODEL), lambda i, j: (i, j)),
        out_shape=jax.ShapeDtypeStruct((M, 2 * D_MODEL), BF16),
        compiler_params=_params("arbitrary", "arbitrary"),
        name="mem_kv",
    )(mem2, xa_wkv)


def _tail_kernel(x_ref, kv_ref, wq_ref, wo_ref, g2_ref, b2_ref, wi_ref, wf_ref, g3_ref, b3_ref,
                 o_ref):
    x1 = x_ref[...]
    q = (jnp.dot(x1.astype(BF16), wq_ref[...], preferred_element_type=F32)
         * (XA_HEAD_DIM ** -0.5)).astype(BF16)
    heads = []
    for h in range(XA_HEADS):
        sl = slice(h * XA_HEAD_DIM, (h + 1) * XA_HEAD_DIM)
        kh = kv_ref[0, :, sl]
        vh = kv_ref[0, :, D_MODEL + h * XA_HEAD_DIM:D_MODEL + (h + 1) * XA_HEAD_DIM]
        s = lax.dot_general(q[:, sl], kh, (((1,), (1,)), ((), ())), preferred_element_type=F32)
        e = jnp.exp(s - jnp.max(s, axis=-1, keepdims=True))
        p = e / jnp.sum(e, axis=-1, keepdims=True)
        heads.append(jnp.dot(p.astype(BF16), vh, preferred_element_type=F32).astype(BF16))
    xa = jnp.dot(jnp.concatenate(heads, axis=-1), wo_ref[...], preferred_element_type=F32)
    x2 = _layer_norm(DEEPNORM_ALPHA * x1 + xa, g2_ref[...], b2_ref[...])
    x2b = x2.astype(BF16)
    y = None
    for c0, cw in FFN_CHUNKS:
        g = jnp.dot(x2b, wi_ref[:, c0:c0 + cw], preferred_element_type=F32)
        u = jnp.dot(x2b, wi_ref[:, D_FF + c0:D_FF + c0 + cw], preferred_element_type=F32)
        hcb = (g * jax.nn.sigmoid(g) * u).astype(BF16)
        part = jnp.dot(hcb, wf_ref[c0:c0 + cw, :], preferred_element_type=F32)
        y = part if y is None else y + part
    o_ref[...] = _layer_norm(DEEPNORM_ALPHA * x2 + y, g3_ref[...], b3_ref[...])


def _tail(x1, kv3, T, xa_wq, xa_wo, ln2_g, ln2_b, ffn_w_in, ffn_w_out, ln3_g, ln3_b):
    M = x1.shape[0]
    tm = ROW_TILE
    tpb = T // tm
    row = pl.BlockSpec((tm, D_MODEL), lambda i: (i, 0))
    return pl.pallas_call(
        _tail_kernel,
        grid=(M // tm,),
        in_specs=[
            row,
            pl.BlockSpec((1, N_MEM, 2 * D_MODEL), lambda i: (i // tpb, 0, 0)),
            _const_spec(xa_wq.shape), _const_spec(xa_wo.shape),
            _const_spec(ln2_g.shape), _const_spec(ln2_b.shape),
            _const_spec(ffn_w_in.shape), _const_spec(ffn_w_out.shape),
            _const_spec(ln3_g.shape), _const_spec(ln3_b.shape),
        ],
        out_specs=row,
        out_shape=jax.ShapeDtypeStruct((M, D_MODEL), F32),
        compiler_params=_params("arbitrary"),
        name="xattn_ffn",
    )(x1, kv3, xa_wq, xa_wo, ln2_g, ln2_b, ffn_w_in, ffn_w_out, ln3_g, ln3_b)


def _rope_tables(T):
    inv = ROPE_THETA ** (-jnp.arange(0, ROT_DIM, 2, dtype=F32) / ROT_DIM)
    ang = jnp.arange(T, dtype=F32)[:, None] * inv[None, :]
    cos, sin = jnp.cos(ang), jnp.sin(ang)
    ones = jnp.ones((T, DA_HEAD_DIM - ROT_DIM), F32)
    zeros = jnp.zeros((T, DA_HEAD_DIM - ROT_DIM), F32)
    z8 = jnp.zeros((T, ROT_HALF), F32)
    cos64 = jnp.concatenate([cos, cos, ones], axis=1)
    sa64 = jnp.concatenate([z8, sin, zeros], axis=1)
    sb64 = jnp.concatenate([-sin, z8, zeros], axis=1)
    rep = LANES_V7X // DA_HEAD_DIM
    return (jnp.tile(cos64, (1, rep)), jnp.tile(sa64, (1, rep)), jnp.tile(sb64, (1, rep)))


def _trunk(x, mem, w):
    B, T, _ = x.shape
    M = B * T
    x2d = x.reshape(M, D_MODEL)
    cos_t, sa_t, sb_t = _rope_tables(T)
    qk, vt, rest = _proj(x2d, w["w_in"], cos_t, sa_t, sb_t, T)
    hf, hb = _lru(rest.reshape(B, T, 4 * D_MODEL), w["conv_w"], w["conv_b"], w["w_gates"],
                  w["b_gates"], w["lru_a"])
    attn = _attention(qk, vt, w["lam4"], w["subln_g"])
    x1 = _merge(attn.reshape(M, D_MODEL), hf.reshape(M, D_MODEL), hb.reshape(M, D_MODEL), rest,
                x2d, w["p_attn"], w["p_lru"], w["w_mix_out"], w["ln1_g"], w["ln1_b"])
    kv = _memkv(mem.reshape(B * N_MEM, D_MODEL), w["xa_wkv"])
    y = _tail(x1, kv.reshape(B, N_MEM, 2 * D_MODEL), T, w["xa_wq"], w["xa_wo"], w["ln2_g"],
              w["ln2_b"], w["ffn_w_in"], w["ffn_w_out"], w["ln3_g"], w["ln3_b"])
    return y.reshape(B, T, D_MODEL)


def _prepare_weights(w_in, lambda_q1, lambda_k1, lambda_q2, lambda_k2, subln_g, conv_w, conv_b,
                     lru_wa, lru_ba, lru_wx, lru_bx, lru_a, p_attn, p_lru, w_mix_out, ln1_g, ln1_b,
                     xa_wq, xa_wkv, xa_wo, ln2_g, ln2_b, ffn_w_in, ffn_w_out, ln3_g, ln3_b):
    row = lambda v: v[0].reshape(1, -1).astype(F32)
    return {
        "w_in": w_in[0].astype(BF16),
        "lam4": jnp.stack([lambda_q1[0], lambda_k1[0], lambda_q2[0], lambda_k2[0]]).astype(F32),
        "subln_g": row(subln_g),
        "conv_w": conv_w[0].astype(F32),
        "conv_b": row(conv_b),
        "w_gates": jnp.concatenate([lru_wa[0], lru_wx[0]], axis=-1).astype(BF16),
        "b_gates": jnp.concatenate([lru_ba[0], lru_bx[0]], axis=-1).astype(F32),
        "lru_a": lru_a[0].astype(F32),
        "p_attn": p_attn[0].astype(BF16),
        "p_lru": p_lru[0].astype(BF16),
        "w_mix_out": w_mix_out[0].astype(BF16),
        "ln1_g": row(ln1_g), "ln1_b": row(ln1_b),
        "xa_wq": xa_wq[0].astype(BF16),
        "xa_wkv": xa_wkv[0].astype(BF16),
        "xa_wo": xa_wo[0].astype(BF16),
        "ln2_g": row(ln2_g), "ln2_b": row(ln2_b),
        "ffn_w_in": ffn_w_in[0].astype(BF16),
        "ffn_w_out": ffn_w_out[0].astype(BF16),
        "ln3_g": row(ln3_g), "ln3_b": row(ln3_b),
    }


def kernel(x_prompt, x_sample, mem_prompt, mem_sample, w_in, lambda_q1, lambda_k1, lambda_q2, lambda_k2, subln_g, conv_w, conv_b, lru_wa, lru_ba, lru_wx, lru_bx, lru_a, p_attn, p_lru, w_mix_out, ln1_g, ln1_b, xa_wq, xa_wkv, xa_wo, ln2_g, ln2_b, ffn_w_in, ffn_w_out, ln3_g, ln3_b):
    w = _prepare_weights(w_in, lambda_q1, lambda_k1, lambda_q2, lambda_k2, subln_g, conv_w, conv_b,
                         lru_wa, lru_ba, lru_wx, lru_bx, lru_a, p_attn, p_lru, w_mix_out, ln1_g,
                         ln1_b, xa_wq, xa_wkv, xa_wo, ln2_g, ln2_b, ffn_w_in, ffn_w_out, ln3_g, ln3_b)
    return (_trunk(x_prompt, mem_prompt, w), _trunk(x_sample, mem_sample, w))
```

```python
import functools
import math

import jax
import jax.numpy as jnp
from jax import lax
from jax.experimental import pallas as pl
from jax.experimental.pallas import tpu as pltpu

F32 = jnp.float32
BF16 = jnp.bfloat16

D_MODEL = 1024
N_MEM = 256
DA_HEADS = 8
DA_HEAD_DIM = 64
DA_PAIR = 2 * DA_HEAD_DIM
ROT_DIM = DA_HEAD_DIM // 4
ROT_HALF = ROT_DIM // 2
ROPE_THETA = 500000.0
SUBLN_EPS = 1e-5
LRU_BLOCKS = 8
LRU_BLOCK_DIM = D_MODEL // LRU_BLOCKS
CONV_WIDTH = 4
LRU_C = 8.0
N_SEG = 7
XA_HEADS = 4
XA_HEAD_DIM = D_MODEL // XA_HEADS
D_FF = ((8 * D_MODEL + 3 * 256 - 1) // (3 * 256)) * 256
DEPTH = 1
DEEPNORM_ALPHA = (2.0 * DEPTH) ** 0.25
LN_EPS = 1e-5
LAMBDA_INIT = 0.8 - 0.6 * math.exp(-0.3 * 0)

LANES_V7X = 128
SUBLANES_V7X = 8
VMEM_LIMIT_V7X = 56 * 1024 * 1024

PROJ_ROWS = 512
LRU_CHUNK = 512
ATTN_TQ = 512
ATTN_TK = 1024
ROW_TILE = 512
FFN_CHUNKS = ((0, 1024), (1024, 1024), (2048, D_FF - 2048))
NEG_BIG = -1e30
ONES_ROWS = 16


def _params(*sem):
    return pltpu.CompilerParams(dimension_semantics=sem, vmem_limit_bytes=VMEM_LIMIT_V7X)


def _const_spec(shape):
    nd = len(shape)
    return pl.BlockSpec(shape, lambda *_: (0,) * nd, pipeline_mode=pl.Buffered(1))


def _layer_norm(x, g, b):
    mu = jnp.mean(x, axis=-1, keepdims=True)
    xc = x - mu
    var = jnp.mean(xc * xc, axis=-1, keepdims=True)
    return xc * lax.rsqrt(var + LN_EPS) * g + b


def _proj_kernel(x_ref, w_ref, cos_ref, sa_ref, sb_ref, qk_ref, vt_ref, xr_ref, act_ref):
    xb = x_ref[...].astype(BF16)

    def matmul(seg):
        return jnp.dot(xb, w_ref[:, seg * D_MODEL:(seg + 1) * D_MODEL], preferred_element_type=F32)

    def rope_store(acc, scale, base):
        c, sa, sb = cos_ref[...], sa_ref[...], sb_ref[...]
        for h in range(DA_HEADS):
            blk = acc[:, h * DA_PAIR:(h + 1) * DA_PAIR]
            r = (blk * c + pltpu.roll(blk, ROT_HALF, 1) * sa
                 + pltpu.roll(blk, LANES_V7X - ROT_HALF, 1) * sb)
            qk_ref[base + h] = (r * scale).astype(BF16)

    rope_store(matmul(0), DA_HEAD_DIM ** -0.5 * math.log2(math.e), 0)
    rope_store(matmul(1), 1.0, DA_HEADS)
    acc = matmul(2)
    ones = jnp.ones((ONES_ROWS, acc.shape[0]), BF16)
    for h in range(DA_HEADS):
        vt_ref[0, h, 0:DA_PAIR, :] = acc[:, h * DA_PAIR:(h + 1) * DA_PAIR].T.astype(BF16)
        vt_ref[0, h, DA_PAIR:DA_PAIR + ONES_ROWS, :] = ones
    xr_ref[...] = matmul(3)
    act_ref[:, 0:D_MODEL] = jax.nn.gelu(matmul(4)).astype(BF16)
    act_ref[:, D_MODEL:2 * D_MODEL] = jax.nn.sigmoid(matmul(5)).astype(BF16)
    act_ref[:, 2 * D_MODEL:3 * D_MODEL] = jax.nn.sigmoid(matmul(6)).astype(BF16)


def _proj(x2d, w_in, cos_t, sa_t, sb_t, T):
    M = x2d.shape[0]
    tm = min(PROJ_ROWS, T)
    tpb = T // tm
    return pl.pallas_call(
        _proj_kernel,
        grid=(M // tm,),
        in_specs=[
            pl.BlockSpec((tm, D_MODEL), lambda i: (i, 0)),
            _const_spec(w_in.shape),
            pl.BlockSpec((tm, LANES_V7X), lambda i: (i % tpb, 0)),
            pl.BlockSpec((tm, LANES_V7X), lambda i: (i % tpb, 0)),
            pl.BlockSpec((tm, LANES_V7X), lambda i: (i % tpb, 0)),
        ],
        out_specs=[
            pl.BlockSpec((2 * DA_HEADS, tm, DA_PAIR), lambda i: (0, i, 0)),
            pl.BlockSpec((1, DA_HEADS, DA_PAIR + ONES_ROWS, tm),
                         lambda i: (i // tpb, 0, 0, i % tpb)),
            pl.BlockSpec((tm, D_MODEL), lambda i: (i, 0)),
            pl.BlockSpec((tm, 3 * D_MODEL), lambda i: (i, 0)),
        ],
        out_shape=[
            jax.ShapeDtypeStruct((2 * DA_HEADS, M, DA_PAIR), BF16),
            jax.ShapeDtypeStruct((M // T, DA_HEADS, DA_PAIR + ONES_ROWS, T), BF16),
            jax.ShapeDtypeStruct((M, D_MODEL), F32),
            jax.ShapeDtypeStruct((M, 3 * D_MODEL), BF16),
        ],
        compiler_params=_params("arbitrary"),
        name="proj",
    )(x2d, w_in, cos_t, sa_t, sb_t)


def _lru_kernel(xf_ref, xfp_ref, xfn_ref, xr_ref, xrp_ref, xrn_ref, cw_ref, cb_ref, w_ref,
                bias_ref, la_ref, hf_ref, hb_ref, af_ref, ab_ref, carry_ref, *, tc):
    c = pl.program_id(1)
    n_c = pl.num_programs(1)

    @pl.when(c == 0)
    def _():
        carry_ref[...] = jnp.zeros_like(carry_ref)

    n_ext = tc + 2 * SUBLANES_V7X

    def prepare(x_ref, xp_ref, xn_ref, d, has_prev, has_next, at_start, start_row, a_ref, u_ref):
        prev = jnp.where(has_prev, xp_ref[0], 0.0)
        nxt = jnp.where(has_next, xn_ref[0], 0.0)
        xall = jnp.concatenate([prev, x_ref[0], nxt], axis=0)
        conv = (cw_ref[0:1, :] * pltpu.roll(xall, 2, 0)
                + cw_ref[1:2, :] * pltpu.roll(xall, 1, 0)
                + cw_ref[2:3, :] * xall
                + cw_ref[3:4, :] * pltpu.roll(xall, n_ext - 1, 0))
        xc = conv[SUBLANES_V7X:SUBLANES_V7X + tc] + cb_ref[...]
        xcb = xc.astype(BF16)
        lam = la_ref[d:d + 1, :]
        decay = -LRU_C * (jnp.maximum(-lam, 0.0) + jnp.log1p(jnp.exp(-jnp.abs(lam))))
        for n in range(LRU_BLOCKS):
            sl = slice(n * LRU_BLOCK_DIM, (n + 1) * LRU_BLOCK_DIM)
            y = jnp.dot(xcb[:, sl], w_ref[d, n], preferred_element_type=F32)
            r = jax.nn.sigmoid(y[:, :LRU_BLOCK_DIM] + bias_ref[d:d + 1, sl])
            gate = jax.nn.sigmoid(
                y[:, LRU_BLOCK_DIM:]
                + bias_ref[d:d + 1, D_MODEL + n * LRU_BLOCK_DIM:D_MODEL + (n + 1) * LRU_BLOCK_DIM])
            a = jnp.exp(r * decay[:, sl])
            m2 = 1.0 - a * a
            mult = jnp.where(m2 > 0.0, m2 * lax.rsqrt(m2), 0.0)
            gx = gate * xc[:, sl]
            a_ref[:, sl] = a
            u_ref[0, :, sl] = mult * gx
            u_ref[0, start_row:start_row + 1, sl] = jnp.where(
                at_start, gx[start_row:start_row + 1], (mult * gx)[start_row:start_row + 1])

    first = c == 0
    last = c == n_c - 1
    prepare(xf_ref, xfp_ref, xfn_ref, 0, jnp.logical_not(first), jnp.logical_not(last),
            first, 0, af_ref, hf_ref)
    prepare(xr_ref, xrp_ref, xrn_ref, 1, jnp.logical_not(last), jnp.logical_not(first),
            first, tc - 1, ab_ref, hb_ref)

    hf, hb = carry_ref[0:1, :], carry_ref[1:2, :]
    for t in range(tc):
        hf = af_ref[t:t + 1, :] * hf + hf_ref[0, t:t + 1, :]
        hf_ref[0, t:t + 1, :] = hf
        tb = tc - 1 - t
        hb = ab_ref[tb:tb + 1, :] * hb + hb_ref[0, tb:tb + 1, :]
        hb_ref[0, tb:tb + 1, :] = hb
    carry_ref[0:1, :] = hf
    carry_ref[1:2, :] = hb


def _lru(xr3, conv_w, conv_b, w_gates, b_gates, lru_a):
    B, T, _ = xr3.shape
    tc = min(LRU_CHUNK, T)
    n_c = T // tc
    hb8 = tc // SUBLANES_V7X
    n8 = T // SUBLANES_V7X

    def cur(f):
        return pl.BlockSpec((1, tc, D_MODEL), lambda b, c: (b, f(c, n_c), 0))

    def prev(f):
        return pl.BlockSpec((1, SUBLANES_V7X, D_MODEL),
                            lambda b, c: (b, jnp.maximum(f(c, n_c) * hb8 - 1, 0), 0))

    def nxt(f):
        return pl.BlockSpec((1, SUBLANES_V7X, D_MODEL),
                            lambda b, c: (b, jnp.minimum((f(c, n_c) + 1) * hb8, n8 - 1), 0))

    fwd = lambda c, n: c
    bwd = lambda c, n: n - 1 - c
    return pl.pallas_call(
        functools.partial(_lru_kernel, tc=tc),
        grid=(B, n_c),
        in_specs=[
            cur(fwd), prev(fwd), nxt(fwd), cur(bwd), prev(bwd), nxt(bwd),
            _const_spec(conv_w.shape), _const_spec(conv_b.shape), _const_spec(w_gates.shape),
            _const_spec(b_gates.shape), _const_spec(lru_a.shape),
        ],
        out_specs=[
            pl.BlockSpec((1, tc, D_MODEL), lambda b, c: (b, c, 0)),
            pl.BlockSpec((1, tc, D_MODEL), lambda b, c: (b, n_c - 1 - c, 0)),
        ],
        out_shape=[jax.ShapeDtypeStruct((B, T, D_MODEL), F32)] * 2,
        scratch_shapes=[
            pltpu.VMEM((tc, D_MODEL), F32),
            pltpu.VMEM((tc, D_MODEL), F32),
            pltpu.VMEM((2, D_MODEL), F32),
        ],
        compiler_params=_params("arbitrary", "arbitrary"),
        name="lru",
    )(xr3, xr3, xr3, xr3, xr3, xr3, conv_w, conv_b, w_gates, b_gates, lru_a)


def _attn_kernel(lam_ref, g_ref, q_ref, k_ref, vt_ref, o_ref, qt_ref, st_ref, acc_ref, m_ref, *,
                 n_q, n_kv, tq, tk):
    def prep_q(qi, slot):
        off = pl.multiple_of(qi * tq, tq)
        qt = q_ref[0, pl.ds(off, tq), :].astype(F32).T
        row = lax.broadcasted_iota(jnp.int32, qt.shape, 0)
        qt_ref[slot] = jnp.concatenate([jnp.where(row < DA_HEAD_DIM, qt, 0.0),
                                        jnp.where(row >= DA_HEAD_DIM, qt, 0.0)], axis=1).astype(BF16)

    def scores(qslot, c, slot):
        off = pl.multiple_of(c * tk, tk)
        st_ref[slot] = jnp.dot(k_ref[0, pl.ds(off, tk), :], qt_ref[qslot],
                               preferred_element_type=F32)

    def accumulate(c, slot):
        off = pl.multiple_of(c * tk, tk)
        st = st_ref[slot]
        m_old = m_ref[...]
        m_new = jnp.maximum(m_old, jnp.max(st, axis=0, keepdims=True))
        alpha = jnp.exp2(m_old - m_new)
        pt = jnp.exp2(st - m_new).astype(BF16)
        acc_ref[...] = alpha * acc_ref[...] + jnp.dot(vt_ref[0, 0, :, pl.ds(off, tk)], pt,
                                                     preferred_element_type=F32)
        m_ref[...] = m_new

    lam = (jnp.exp(jnp.sum(lam_ref[0:1, :] * lam_ref[1:2, :], axis=-1, keepdims=True))
           - jnp.exp(jnp.sum(lam_ref[2:3, :] * lam_ref[3:4, :], axis=-1, keepdims=True))
           + LAMBDA_INIT)

    def finalize(qi):
        ot = (acc_ref[0:DA_PAIR, 0:tq] / acc_ref[DA_PAIR:DA_PAIR + 1, 0:tq]
              - lam * (acc_ref[0:DA_PAIR, tq:2 * tq] / acc_ref[DA_PAIR:DA_PAIR + 1, tq:2 * tq]))
        o = ot.T
        o = o * lax.rsqrt(jnp.mean(o * o, axis=-1, keepdims=True) + SUBLN_EPS) * g_ref[...]
        off = pl.multiple_of(qi * tq, tq)
        o_ref[0, pl.ds(off, tq), :] = (o * (1.0 - LAMBDA_INIT)).astype(BF16)

    prep_q(0, 0)
    scores(0, 0, 0)

    def q_block(qi, _):
        qs = lax.rem(qi, 2)
        acc_ref[...] = jnp.zeros_like(acc_ref)
        m_ref[...] = jnp.full(m_ref.shape, NEG_BIG, F32)

        def pair(j, _):
            scores(qs, 2 * j + 1, 1)
            accumulate(2 * j, 0)
            scores(qs, 2 * j + 2, 0)
            accumulate(2 * j + 1, 1)
            return 0

        lax.fori_loop(0, n_kv // 2 - 1, pair, 0)
        scores(qs, n_kv - 1, 1)
        accumulate(n_kv - 2, 0)
        prep_q(jnp.minimum(qi + 1, n_q - 1), 1 - qs)
        scores(1 - qs, 0, 0)
        accumulate(n_kv - 1, 1)
        finalize(qi)
        return 0

    lax.fori_loop(0, n_q, q_block, 0)


def _attention(qk, vt, lam4, subln_g):
    B, _, _, T = vt.shape
    tq = min(ATTN_TQ, T)
    tk = min(ATTN_TK, T)
    assert T % (2 * tk) == 0 and T % tq == 0, "key chunks are consumed in pairs"
    return pl.pallas_call(
        functools.partial(_attn_kernel, n_q=T // tq, n_kv=T // tk, tq=tq, tk=tk),
        grid=(B, DA_HEADS),
        in_specs=[
            _const_spec(lam4.shape), _const_spec(subln_g.shape),
            pl.BlockSpec((1, T, DA_PAIR), lambda b, h: (h, b, 0)),
            pl.BlockSpec((1, T, DA_PAIR), lambda b, h: (DA_HEADS + h, b, 0)),
            pl.BlockSpec((1, 1, DA_PAIR + ONES_ROWS, T), lambda b, h: (b, h, 0, 0)),
        ],
        out_specs=pl.BlockSpec((1, T, DA_PAIR), lambda b, h: (b, 0, h)),
        out_shape=jax.ShapeDtypeStruct((B, T, D_MODEL), BF16),
        scratch_shapes=[
            pltpu.VMEM((2, DA_PAIR, 2 * tq), BF16),
            pltpu.VMEM((2, tk, 2 * tq), F32),
            pltpu.VMEM((DA_PAIR + ONES_ROWS, 2 * tq), F32),
            pltpu.VMEM((1, 2 * tq), F32),
        ],
        compiler_params=_params("arbitrary", "arbitrary"),
        name="diff_attn",
    )(lam4, subln_g, qk, qk, vt)


def _merge_kernel(attn_ref, hf_ref, hb_ref, gy_ref, sa_ref, sl_ref, x_ref, pa_ref, plru_ref,
                  wo_ref, g_ref, b_ref, o_ref):
    a_proj = jnp.dot(attn_ref[...], pa_ref[...], preferred_element_type=F32)
    lru_out = ((hf_ref[...] + hb_ref[...]) * gy_ref[...].astype(F32)).astype(BF16)
    l_proj = jnp.dot(lru_out, plru_ref[...], preferred_element_type=F32)
    merged = sa_ref[...].astype(F32) * a_proj + sl_ref[...].astype(F32) * l_proj
    m = jnp.dot(merged.astype(BF16), wo_ref[...], preferred_element_type=F32)
    o_ref[...] = _layer_norm(DEEPNORM_ALPHA * x_ref[...] + m, g_ref[...], b_ref[...])


def _merge(attn2, hf2, hb2, act2, x2d, p_attn, p_lru, w_mix_out, ln_g, ln_b):
    M = x2d.shape[0]
    tm = ROW_TILE
    row = lambda seg: pl.BlockSpec((tm, D_MODEL), lambda i: (i, seg))
    return pl.pallas_call(
        _merge_kernel,
        grid=(M // tm,),
        in_specs=[
            row(0), row(0), row(0), row(0), row(1), row(2), row(0),
            _const_spec(p_attn.shape), _const_spec(p_lru.shape), _const_spec(w_mix_out.shape),
            _const_spec(ln_g.shape), _const_spec(ln_b.shape),
        ],
        out_specs=row(0),
        out_shape=jax.ShapeDtypeStruct((M, D_MODEL), F32),
        compiler_params=_params("arbitrary"),
        name="merge_ln1",
    )(attn2, hf2, hb2, act2, act2, act2, x2d, p_attn, p_lru, w_mix_out, ln_g, ln_b)


def _memkv_kernel(m_ref, w_ref, o_ref):
    o_ref[...] = jnp.dot(m_ref[...].astype(BF16), w_ref[...],
                         preferred_element_type=F32).astype(BF16)


def _memkv(mem2, xa_wkv):
    M = mem2.shape[0]
    tm = min(ROW_TILE, M)
    return pl.pallas_call(
        _memkv_kernel,
        grid=(M // tm, 2),
        in_specs=[
            pl.BlockSpec((tm, D_MODEL), lambda i, j: (i, 0)),
            pl.BlockSpec((D_MODEL, D_MODEL), lambda i, j: (0, j)),
        ],
        out_specs=pl.BlockSpec((tm, D_MODEL), lambda i, j: (i, j)),
        out_shape=jax.ShapeDtypeStruct((M, 2 * D_MODEL), BF16),
        compiler_params=_params("arbitrary", "arbitrary"),
        name="mem_kv",
    )(mem2, xa_wkv)


def _tail_kernel(x_ref, kv_ref, wq_ref, wo_ref, g2_ref, b2_ref, wi_ref, wf_ref, g3_ref, b3_ref,
                 o_ref):
    x1 = x_ref[...]
    q = (jnp.dot(x1.astype(BF16), wq_ref[...], preferred_element_type=F32)
         * (XA_HEAD_DIM ** -0.5)).astype(BF16)
    heads = []
    for h in range(XA_HEADS):
        sl = slice(h * XA_HEAD_DIM, (h + 1) * XA_HEAD_DIM)
        kh = kv_ref[0, :, sl]
        vh = kv_ref[0, :, D_MODEL + h * XA_HEAD_DIM:D_MODEL + (h + 1) * XA_HEAD_DIM]
        s = lax.dot_general(q[:, sl], kh, (((1,), (1,)), ((), ())), preferred_element_type=F32)
        e = jnp.exp(s - jnp.max(s, axis=-1, keepdims=True))
        p = e / jnp.sum(e, axis=-1, keepdims=True)
        heads.append(jnp.dot(p.astype(BF16), vh, preferred_element_type=F32).astype(BF16))
    xa = jnp.dot(jnp.concatenate(heads, axis=-1), wo_ref[...], preferred_element_type=F32)
    x2 = _layer_norm(DEEPNORM_ALPHA * x1 + xa, g2_ref[...], b2_ref[...])
    x2b = x2.astype(BF16)
    y = None
    for c0, cw in FFN_CHUNKS:
        g = jnp.dot(x2b, wi_ref[:, c0:c0 + cw], preferred_element_type=F32)
        u = jnp.dot(x2b, wi_ref[:, D_FF + c0:D_FF + c0 + cw], preferred_element_type=F32)
        hcb = (g * jax.nn.sigmoid(g) * u).astype(BF16)
        part = jnp.dot(hcb, wf_ref[c0:c0 + cw, :], preferred_element_type=F32)
        y = part if y is None else y + part
    o_ref[...] = _layer_norm(DEEPNORM_ALPHA * x2 + y, g3_ref[...], b3_ref[...])


def _tail(x1, kv3, T, xa_wq, xa_wo, ln2_g, ln2_b, ffn_w_in, ffn_w_out, ln3_g, ln3_b):
    M = x1.shape[0]
    tm = ROW_TILE
    tpb = T // tm
    row = pl.BlockSpec((tm, D_MODEL), lambda i: (i, 0))
    return pl.pallas_call(
        _tail_kernel,
        grid=(M // tm,),
        in_specs=[
            row,
            pl.BlockSpec((1, N_MEM, 2 * D_MODEL), lambda i: (i // tpb, 0, 0)),
            _const_spec(xa_wq.shape), _const_spec(xa_wo.shape),
            _const_spec(ln2_g.shape), _const_spec(ln2_b.shape),
            _const_spec(ffn_w_in.shape), _const_spec(ffn_w_out.shape),
            _const_spec(ln3_g.shape), _const_spec(ln3_b.shape),
        ],
        out_specs=row,
        out_shape=jax.ShapeDtypeStruct((M, D_MODEL), F32),
        compiler_params=_params("arbitrary"),
        name="xattn_ffn",
    )(x1, kv3, xa_wq, xa_wo, ln2_g, ln2_b, ffn_w_in, ffn_w_out, ln3_g, ln3_b)


def _rope_tables(T):
    inv = ROPE_THETA ** (-jnp.arange(0, ROT_DIM, 2, dtype=F32) / ROT_DIM)
    ang = jnp.arange(T, dtype=F32)[:, None] * inv[None, :]
    cos, sin = jnp.cos(ang), jnp.sin(ang)
    ones = jnp.ones((T, DA_HEAD_DIM - ROT_DIM), F32)
    zeros = jnp.zeros((T, DA_HEAD_DIM - ROT_DIM), F32)
    z8 = jnp.zeros((T, ROT_HALF), F32)
    cos64 = jnp.concatenate([cos, cos, ones], axis=1)
    sa64 = jnp.concatenate([z8, sin, zeros], axis=1)
    sb64 = jnp.concatenate([-sin, z8, zeros], axis=1)
    rep = LANES_V7X // DA_HEAD_DIM
    return (jnp.tile(cos64, (1, rep)), jnp.tile(sa64, (1, rep)), jnp.tile(sb64, (1, rep)))


def _trunk(x, mem, w):
    B, T, _ = x.shape
    M = B * T
    x2d = x.reshape(M, D_MODEL)
    cos_t, sa_t, sb_t = _rope_tables(T)
    qk, vt, xr, act = _proj(x2d, w["w_in"], cos_t, sa_t, sb_t, T)
    hf, hb = _lru(xr.reshape(B, T, D_MODEL), w["conv_w"], w["conv_b"], w["w_gates"],
                  w["b_gates"], w["lru_a"])
    attn = _attention(qk, vt, w["lam4"], w["subln_g"])
    x1 = _merge(attn.reshape(M, D_MODEL), hf.reshape(M, D_MODEL), hb.reshape(M, D_MODEL), act,
                x2d, w["p_attn"], w["p_lru"], w["w_mix_out"], w["ln1_g"], w["ln1_b"])
    kv = _memkv(mem.reshape(B * N_MEM, D_MODEL), w["xa_wkv"])
    y = _tail(x1, kv.reshape(B, N_MEM, 2 * D_MODEL), T, w["xa_wq"], w["xa_wo"], w["ln2_g"],
              w["ln2_b"], w["ffn_w_in"], w["ffn_w_out"], w["ln3_g"], w["ln3_b"])
    return y.reshape(B, T, D_MODEL)


def _prepare_weights(w_in, lambda_q1, lambda_k1, lambda_q2, lambda_k2, subln_g, conv_w, conv_b,
                     lru_wa, lru_ba, lru_wx, lru_bx, lru_a, p_attn, p_lru, w_mix_out, ln1_g, ln1_b,
                     xa_wq, xa_wkv, xa_wo, ln2_g, ln2_b, ffn_w_in, ffn_w_out, ln3_g, ln3_b):
    row = lambda v: v[0].reshape(1, -1).astype(F32)
    return {
        "w_in": w_in[0].astype(BF16),
        "lam4": jnp.stack([lambda_q1[0], lambda_k1[0], lambda_q2[0], lambda_k2[0]]).astype(F32),
        "subln_g": row(subln_g),
        "conv_w": conv_w[0].astype(F32),
        "conv_b": row(conv_b),
        "w_gates": jnp.concatenate([lru_wa[0], lru_wx[0]], axis=-1).astype(BF16),
        "b_gates": jnp.concatenate([lru_ba[0], lru_bx[0]], axis=-1).astype(F32),
        "lru_a": lru_a[0].astype(F32),
        "p_attn": p_attn[0].astype(BF16),
        "p_lru": p_lru[0].astype(BF16),
        "w_mix_out": w_mix_out[0].astype(BF16),
        "ln1_g": row(ln1_g), "ln1_b": row(ln1_b),
        "xa_wq": xa_wq[0].astype(BF16),
        "xa_wkv": xa_wkv[0].astype(BF16),
        "xa_wo": xa_wo[0].astype(BF16),
        "ln2_g": row(ln2_g), "ln2_b": row(ln2_b),
        "ffn_w_in": ffn_w_in[0].astype(BF16),
        "ffn_w_out": ffn_w_out[0].astype(BF16),
        "ln3_g": row(ln3_g), "ln3_b": row(ln3_b),
    }


def kernel(x_prompt, x_sample, mem_prompt, mem_sample, w_in, lambda_q1, lambda_k1, lambda_q2, lambda_k2, subln_g, conv_w, conv_b, lru_wa, lru_ba, lru_wx, lru_bx, lru_a, p_attn, p_lru, w_mix_out, ln1_g, ln1_b, xa_wq, xa_wkv, xa_wo, ln2_g, ln2_b, ffn_w_in, ffn_w_out, ln3_g, ln3_b):
    w = _prepare_weights(w_in, lambda_q1, lambda_k1, lambda_q2, lambda_k2, subln_g, conv_w, conv_b,
                         lru_wa, lru_ba, lru_wx, lru_bx, lru_a, p_attn, p_lru, w_mix_out, ln1_g,
                         ln1_b, xa_wq, xa_wkv, xa_wo, ln2_g, ln2_b, ffn_w_in, ffn_w_out, ln3_g, ln3_b)
    return (_trunk(x_prompt, mem_prompt, w), _trunk(x_sample, mem_sample, w))
```

```python
import functools
import math

import jax
import jax.numpy as jnp
from jax import lax
from jax.experimental import pallas as pl
from jax.experimental.pallas import tpu as pltpu

F32 = jnp.float32
BF16 = jnp.bfloat16

D_MODEL = 1024
N_MEM = 256
DA_HEADS = 8
DA_HEAD_DIM = 64
DA_PAIR = 2 * DA_HEAD_DIM
ROT_DIM = DA_HEAD_DIM // 4
ROT_HALF = ROT_DIM // 2
ROPE_THETA = 500000.0
SUBLN_EPS = 1e-5
LRU_BLOCKS = 8
LRU_BLOCK_DIM = D_MODEL // LRU_BLOCKS
CONV_WIDTH = 4
LRU_C = 8.0
N_SEG = 7
XA_HEADS = 4
XA_HEAD_DIM = D_MODEL // XA_HEADS
D_FF = ((8 * D_MODEL + 3 * 256 - 1) // (3 * 256)) * 256
DEPTH = 1
DEEPNORM_ALPHA = (2.0 * DEPTH) ** 0.25
LN_EPS = 1e-5
LAMBDA_INIT = 0.8 - 0.6 * math.exp(-0.3 * 0)

LANES_V7X = 128
SUBLANES_V7X = 8
VMEM_LIMIT_V7X = 56 * 1024 * 1024

PROJ_ROWS = 512
LRU_CHUNK = 512
ATTN_TQ = 512
ATTN_TK = 1024
ROW_TILE = 512
FFN_CHUNKS = ((0, 1024), (1024, 1024), (2048, D_FF - 2048))
NEG_BIG = -1e30
ONES_ROWS = 16


def _params(*sem):
    return pltpu.CompilerParams(dimension_semantics=sem, vmem_limit_bytes=VMEM_LIMIT_V7X)


def _const_spec(shape):
    nd = len(shape)
    return pl.BlockSpec(shape, lambda *_: (0,) * nd, pipeline_mode=pl.Buffered(1))


def _layer_norm(x, g, b):
    mu = jnp.mean(x, axis=-1, keepdims=True)
    xc = x - mu
    var = jnp.mean(xc * xc, axis=-1, keepdims=True)
    return xc * lax.rsqrt(var + LN_EPS) * g + b


def _proj_kernel(x_ref, w_ref, cos_ref, sa_ref, sb_ref, qk_ref, vt_ref, xr_ref, act_ref):
    xb = x_ref[...].astype(BF16)

    def matmul(seg, nseg):
        return jnp.dot(xb, w_ref[:, seg * D_MODEL:(seg + nseg) * D_MODEL],
                       preferred_element_type=F32)

    def cols(acc, i):
        return acc[:, i * D_MODEL:(i + 1) * D_MODEL]

    def rope_store(acc, scale, base):
        c, sa, sb = cos_ref[...], sa_ref[...], sb_ref[...]
        for h in range(DA_HEADS):
            blk = acc[:, h * DA_PAIR:(h + 1) * DA_PAIR]
            r = (blk * c + pltpu.roll(blk, ROT_HALF, 1) * sa
                 + pltpu.roll(blk, LANES_V7X - ROT_HALF, 1) * sb)
            qk_ref[base + h] = (r * scale).astype(BF16)

    act_ref[...] = matmul(4, 3).astype(BF16)
    qk = matmul(0, 2)
    rope_store(cols(qk, 0), DA_HEAD_DIM ** -0.5 * math.log2(math.e), 0)
    rope_store(cols(qk, 1), 1.0, DA_HEADS)
    vx = matmul(2, 2)
    ones = jnp.ones((ONES_ROWS, vx.shape[0]), BF16)
    for h in range(DA_HEADS):
        vt_ref[0, h, 0:DA_PAIR, :] = vx[:, h * DA_PAIR:(h + 1) * DA_PAIR].T.astype(BF16)
        vt_ref[0, h, DA_PAIR:DA_PAIR + ONES_ROWS, :] = ones
    xr_ref[...] = cols(vx, 1)


def _proj(x2d, w_in, cos_t, sa_t, sb_t, T):
    M = x2d.shape[0]
    tm = min(PROJ_ROWS, T)
    tpb = T // tm
    return pl.pallas_call(
        _proj_kernel,
        grid=(M // tm,),
        in_specs=[
            pl.BlockSpec((tm, D_MODEL), lambda i: (i, 0)),
            _const_spec(w_in.shape),
            pl.BlockSpec((tm, LANES_V7X), lambda i: (i % tpb, 0)),
            pl.BlockSpec((tm, LANES_V7X), lambda i: (i % tpb, 0)),
            pl.BlockSpec((tm, LANES_V7X), lambda i: (i % tpb, 0)),
        ],
        out_specs=[
            pl.BlockSpec((2 * DA_HEADS, tm, DA_PAIR), lambda i: (0, i, 0)),
            pl.BlockSpec((1, DA_HEADS, DA_PAIR + ONES_ROWS, tm),
                         lambda i: (i // tpb, 0, 0, i % tpb)),
            pl.BlockSpec((tm, D_MODEL), lambda i: (i, 0)),
            pl.BlockSpec((tm, 3 * D_MODEL), lambda i: (i, 0)),
        ],
        out_shape=[
            jax.ShapeDtypeStruct((2 * DA_HEADS, M, DA_PAIR), BF16),
            jax.ShapeDtypeStruct((M // T, DA_HEADS, DA_PAIR + ONES_ROWS, T), BF16),
            jax.ShapeDtypeStruct((M, D_MODEL), F32),
            jax.ShapeDtypeStruct((M, 3 * D_MODEL), BF16),
        ],
        compiler_params=_params("arbitrary"),
        name="proj",
    )(x2d, w_in, cos_t, sa_t, sb_t)


def _lru_kernel(xf_ref, xfp_ref, xfn_ref, xr_ref, xrp_ref, xrn_ref, cw_ref, cb_ref, w_ref,
                bias_ref, la_ref, hf_ref, hb_ref, af_ref, ab_ref, carry_ref, *, tc):
    c = pl.program_id(1)
    n_c = pl.num_programs(1)

    @pl.when(c == 0)
    def _():
        carry_ref[...] = jnp.zeros_like(carry_ref)

    n_ext = tc + 2 * SUBLANES_V7X

    def prepare(x_ref, xp_ref, xn_ref, d, has_prev, has_next, at_start, start_row, a_ref, u_ref):
        prev = jnp.where(has_prev, xp_ref[0], 0.0)
        nxt = jnp.where(has_next, xn_ref[0], 0.0)
        xall = jnp.concatenate([prev, x_ref[0], nxt], axis=0)
        conv = (cw_ref[0:1, :] * pltpu.roll(xall, 2, 0)
                + cw_ref[1:2, :] * pltpu.roll(xall, 1, 0)
                + cw_ref[2:3, :] * xall
                + cw_ref[3:4, :] * pltpu.roll(xall, n_ext - 1, 0))
        xc = conv[SUBLANES_V7X:SUBLANES_V7X + tc] + cb_ref[...]
        xcb = xc.astype(BF16)
        lam = la_ref[d:d + 1, :]
        decay = -LRU_C * (jnp.maximum(-lam, 0.0) + jnp.log1p(jnp.exp(-jnp.abs(lam))))
        for n in range(LRU_BLOCKS):
            sl = slice(n * LRU_BLOCK_DIM, (n + 1) * LRU_BLOCK_DIM)
            y = jnp.dot(xcb[:, sl], w_ref[d, n], preferred_element_type=F32)
            r = jax.nn.sigmoid(y[:, :LRU_BLOCK_DIM] + bias_ref[d:d + 1, sl])
            gate = jax.nn.sigmoid(
                y[:, LRU_BLOCK_DIM:]
                + bias_ref[d:d + 1, D_MODEL + n * LRU_BLOCK_DIM:D_MODEL + (n + 1) * LRU_BLOCK_DIM])
            a = jnp.exp(r * decay[:, sl])
            m2 = 1.0 - a * a
            mult = jnp.where(m2 > 0.0, m2 * lax.rsqrt(m2), 0.0)
            gx = gate * xc[:, sl]
            a_ref[:, sl] = a
            u_ref[0, :, sl] = mult * gx
            u_ref[0, start_row:start_row + 1, sl] = jnp.where(
                at_start, gx[start_row:start_row + 1], (mult * gx)[start_row:start_row + 1])

    first = c == 0
    last = c == n_c - 1
    prepare(xf_ref, xfp_ref, xfn_ref, 0, jnp.logical_not(first), jnp.logical_not(last),
            first, 0, af_ref, hf_ref)
    prepare(xr_ref, xrp_ref, xrn_ref, 1, jnp.logical_not(last), jnp.logical_not(first),
            first, tc - 1, ab_ref, hb_ref)

    hf, hb = carry_ref[0:1, :], carry_ref[1:2, :]
    for t in range(tc):
        hf = af_ref[t:t + 1, :] * hf + hf_ref[0, t:t + 1, :]
        hf_ref[0, t:t + 1, :] = hf
        tb = tc - 1 - t
        hb = ab_ref[tb:tb + 1, :] * hb + hb_ref[0, tb:tb + 1, :]
        hb_ref[0, tb:tb + 1, :] = hb
    carry_ref[0:1, :] = hf
    carry_ref[1:2, :] = hb


def _lru(xr3, conv_w, conv_b, w_gates, b_gates, lru_a):
    B, T, _ = xr3.shape
    tc = min(LRU_CHUNK, T)
    n_c = T // tc
    hb8 = tc // SUBLANES_V7X
    n8 = T // SUBLANES_V7X

    def cur(f):
        return pl.BlockSpec((1, tc, D_MODEL), lambda b, c: (b, f(c, n_c), 0))

    def prev(f):
        return pl.BlockSpec((1, SUBLANES_V7X, D_MODEL),
                            lambda b, c: (b, jnp.maximum(f(c, n_c) * hb8 - 1, 0), 0))

    def nxt(f):
        return pl.BlockSpec((1, SUBLANES_V7X, D_MODEL),
                            lambda b, c: (b, jnp.minimum((f(c, n_c) + 1) * hb8, n8 - 1), 0))

    fwd = lambda c, n: c
    bwd = lambda c, n: n - 1 - c
    return pl.pallas_call(
        functools.partial(_lru_kernel, tc=tc),
        grid=(B, n_c),
        in_specs=[
            cur(fwd), prev(fwd), nxt(fwd), cur(bwd), prev(bwd), nxt(bwd),
            _const_spec(conv_w.shape), _const_spec(conv_b.shape), _const_spec(w_gates.shape),
            _const_spec(b_gates.shape), _const_spec(lru_a.shape),
        ],
        out_specs=[
            pl.BlockSpec((1, tc, D_MODEL), lambda b, c: (b, c, 0)),
            pl.BlockSpec((1, tc, D_MODEL), lambda b, c: (b, n_c - 1 - c, 0)),
        ],
        out_shape=[jax.ShapeDtypeStruct((B, T, D_MODEL), F32)] * 2,
        scratch_shapes=[
            pltpu.VMEM((tc, D_MODEL), F32),
            pltpu.VMEM((tc, D_MODEL), F32),
            pltpu.VMEM((2, D_MODEL), F32),
        ],
        compiler_params=_params("arbitrary", "arbitrary"),
        name="lru",
    )(xr3, xr3, xr3, xr3, xr3, xr3, conv_w, conv_b, w_gates, b_gates, lru_a)


def _attn_kernel(lam_ref, g_ref, q_ref, k_ref, vt_ref, o_ref, qt_ref, st_ref, mc_ref, acc_ref,
                 m_ref, *, n_q, n_kv, tq, tk):
    def prep_q(qi, slot):
        off = pl.multiple_of(qi * tq, tq)
        qt = q_ref[0, pl.ds(off, tq), :].astype(F32).T
        row = lax.broadcasted_iota(jnp.int32, qt.shape, 0)
        qt_ref[slot] = jnp.concatenate([jnp.where(row < DA_HEAD_DIM, qt, 0.0),
                                        jnp.where(row >= DA_HEAD_DIM, qt, 0.0)], axis=1).astype(BF16)

    def scores(qslot, c, slot):
        off = pl.multiple_of(c * tk, tk)
        st = jnp.dot(k_ref[0, pl.ds(off, tk), :], qt_ref[qslot], preferred_element_type=F32)
        st_ref[slot] = st
        mc_ref[slot] = jnp.max(st, axis=0, keepdims=True)

    def accumulate(c, slot):
        off = pl.multiple_of(c * tk, tk)
        st = st_ref[slot]
        m_old = m_ref[...]
        m_new = jnp.maximum(m_old, mc_ref[slot])
        alpha = jnp.exp2(m_old - m_new)
        pt = jnp.exp2(st - m_new).astype(BF16)
        acc_ref[...] = alpha * acc_ref[...] + jnp.dot(vt_ref[0, 0, :, pl.ds(off, tk)], pt,
                                                     preferred_element_type=F32)
        m_ref[...] = m_new

    lam = (jnp.exp(jnp.sum(lam_ref[0:1, :] * lam_ref[1:2, :], axis=-1, keepdims=True))
           - jnp.exp(jnp.sum(lam_ref[2:3, :] * lam_ref[3:4, :], axis=-1, keepdims=True))
           + LAMBDA_INIT)

    def finalize(qi):
        ot = (acc_ref[0:DA_PAIR, 0:tq] / acc_ref[DA_PAIR:DA_PAIR + 1, 0:tq]
              - lam * (acc_ref[0:DA_PAIR, tq:2 * tq] / acc_ref[DA_PAIR:DA_PAIR + 1, tq:2 * tq]))
        o = ot.T
        o = o * lax.rsqrt(jnp.mean(o * o, axis=-1, keepdims=True) + SUBLN_EPS) * g_ref[...]
        off = pl.multiple_of(qi * tq, tq)
        o_ref[0, pl.ds(off, tq), :] = (o * (1.0 - LAMBDA_INIT)).astype(BF16)

    prep_q(0, 0)
    scores(0, 0, 0)

    def q_block(qi, _):
        qs = lax.rem(qi, 2)
        acc_ref[...] = jnp.zeros_like(acc_ref)
        m_ref[...] = jnp.full(m_ref.shape, NEG_BIG, F32)

        def pair(j, _):
            scores(qs, 2 * j + 1, 1)
            accumulate(2 * j, 0)
            scores(qs, 2 * j + 2, 0)
            accumulate(2 * j + 1, 1)
            return 0

        lax.fori_loop(0, n_kv // 2 - 1, pair, 0)
        scores(qs, n_kv - 1, 1)
        accumulate(n_kv - 2, 0)
        prep_q(jnp.minimum(qi + 1, n_q - 1), 1 - qs)
        scores(1 - qs, 0, 0)
        accumulate(n_kv - 1, 1)
        finalize(qi)
        return 0

    lax.fori_loop(0, n_q, q_block, 0)


def _attention(qk, vt, lam4, subln_g):
    B, _, _, T = vt.shape
    tq = min(ATTN_TQ, T)
    tk = min(ATTN_TK, T)
    assert T % (2 * tk) == 0 and T % tq == 0, "key chunks are consumed in pairs"
    return pl.pallas_call(
        functools.partial(_attn_kernel, n_q=T // tq, n_kv=T // tk, tq=tq, tk=tk),
        grid=(B, DA_HEADS),
        in_specs=[
            _const_spec(lam4.shape), _const_spec(subln_g.shape),
            pl.BlockSpec((1, T, DA_PAIR), lambda b, h: (h, b, 0)),
            pl.BlockSpec((1, T, DA_PAIR), lambda b, h: (DA_HEADS + h, b, 0)),
            pl.BlockSpec((1, 1, DA_PAIR + ONES_ROWS, T), lambda b, h: (b, h, 0, 0)),
        ],
        out_specs=pl.BlockSpec((1, T, DA_PAIR), lambda b, h: (b, 0, h)),
        out_shape=jax.ShapeDtypeStruct((B, T, D_MODEL), BF16),
        scratch_shapes=[
            pltpu.VMEM((2, DA_PAIR, 2 * tq), BF16),
            pltpu.VMEM((2, tk, 2 * tq), F32),
            pltpu.VMEM((2, 1, 2 * tq), F32),
            pltpu.VMEM((DA_PAIR + ONES_ROWS, 2 * tq), F32),
            pltpu.VMEM((1, 2 * tq), F32),
        ],
        compiler_params=_params("arbitrary", "arbitrary"),
        name="diff_attn",
    )(lam4, subln_g, qk, qk, vt)


def _merge_kernel(attn_ref, hf_ref, hb_ref, gy_ref, sa_ref, sl_ref, x_ref, pa_ref, plru_ref,
                  wo_ref, g_ref, b_ref, o_ref):
    a_proj = jnp.dot(attn_ref[...], pa_ref[...], preferred_element_type=F32)
    lru_out = ((hf_ref[...] + hb_ref[...]) * jax.nn.gelu(gy_ref[...].astype(F32))).astype(BF16)
    l_proj = jnp.dot(lru_out, plru_ref[...], preferred_element_type=F32)
    merged = (jax.nn.sigmoid(sa_ref[...].astype(F32)) * a_proj
              + jax.nn.sigmoid(sl_ref[...].astype(F32)) * l_proj)
    m = jnp.dot(merged.astype(BF16), wo_ref[...], preferred_element_type=F32)
    o_ref[...] = _layer_norm(DEEPNORM_ALPHA * x_ref[...] + m, g_ref[...], b_ref[...])


def _merge(attn2, hf2, hb2, act2, x2d, p_attn, p_lru, w_mix_out, ln_g, ln_b):
    M = x2d.shape[0]
    tm = ROW_TILE
    row = lambda seg: pl.BlockSpec((tm, D_MODEL), lambda i: (i, seg))
    return pl.pallas_call(
        _merge_kernel,
        grid=(M // tm,),
        in_specs=[
            row(0), row(0), row(0), row(0), row(1), row(2), row(0),
            _const_spec(p_attn.shape), _const_spec(p_lru.shape), _const_spec(w_mix_out.shape),
            _const_spec(ln_g.shape), _const_spec(ln_b.shape),
        ],
        out_specs=row(0),
        out_shape=jax.ShapeDtypeStruct((M, D_MODEL), F32),
        compiler_params=_params("arbitrary"),
        name="merge_ln1",
    )(attn2, hf2, hb2, act2, act2, act2, x2d, p_attn, p_lru, w_mix_out, ln_g, ln_b)


def _memkv_kernel(m_ref, w_ref, o_ref):
    o_ref[...] = jnp.dot(m_ref[...].astype(BF16), w_ref[...],
                         preferred_element_type=F32).astype(BF16)


def _memkv(mem2, xa_wkv):
    M = mem2.shape[0]
    tm = min(ROW_TILE, M)
    return pl.pallas_call(
        _memkv_kernel,
        grid=(M // tm, 2),
        in_specs=[
            pl.BlockSpec((tm, D_MODEL), lambda i, j: (i, 0)),
            pl.BlockSpec((D_MODEL, D_MODEL), lambda i, j: (0, j)),
        ],
        out_specs=pl.BlockSpec((tm, D_MODEL), lambda i, j: (i, j)),
        out_shape=jax.ShapeDtypeStruct((M, 2 * D_MODEL), BF16),
        compiler_params=_params("arbitrary", "arbitrary"),
        name="mem_kv",
    )(mem2, xa_wkv)


def _tail_kernel(x_ref, kv_ref, wq_ref, wo_ref, g2_ref, b2_ref, wi_ref, wf_ref, g3_ref, b3_ref,
                 o_ref):
    x1 = x_ref[...]
    q = (jnp.dot(x1.astype(BF16), wq_ref[...], preferred_element_type=F32)
         * (XA_HEAD_DIM ** -0.5)).astype(BF16)
    heads = []
    for h in range(XA_HEADS):
        sl = slice(h * XA_HEAD_DIM, (h + 1) * XA_HEAD_DIM)
        kh = kv_ref[0, :, sl]
        vh = kv_ref[0, :, D_MODEL + h * XA_HEAD_DIM:D_MODEL + (h + 1) * XA_HEAD_DIM]
        s = lax.dot_general(q[:, sl], kh, (((1,), (1,)), ((), ())), preferred_element_type=F32)
        e = jnp.exp(s - jnp.max(s, axis=-1, keepdims=True))
        p = e / jnp.sum(e, axis=-1, keepdims=True)
        heads.append(jnp.dot(p.astype(BF16), vh, preferred_element_type=F32).astype(BF16))
    xa = jnp.dot(jnp.concatenate(heads, axis=-1), wo_ref[...], preferred_element_type=F32)
    x2 = _layer_norm(DEEPNORM_ALPHA * x1 + xa, g2_ref[...], b2_ref[...])
    x2b = x2.astype(BF16)
    y = None
    for c0, cw in FFN_CHUNKS:
        g = jnp.dot(x2b, wi_ref[:, c0:c0 + cw], preferred_element_type=F32)
        u = jnp.dot(x2b, wi_ref[:, D_FF + c0:D_FF + c0 + cw], preferred_element_type=F32)
        hcb = (g * jax.nn.sigmoid(g) * u).astype(BF16)
        part = jnp.dot(hcb, wf_ref[c0:c0 + cw, :], preferred_element_type=F32)
        y = part if y is None else y + part
    o_ref[...] = _layer_norm(DEEPNORM_ALPHA * x2 + y, g3_ref[...], b3_ref[...])


def _tail(x1, kv3, T, xa_wq, xa_wo, ln2_g, ln2_b, ffn_w_in, ffn_w_out, ln3_g, ln3_b):
    M = x1.shape[0]
    tm = ROW_TILE
    tpb = T // tm
    row = pl.BlockSpec((tm, D_MODEL), lambda i: (i, 0))
    return pl.pallas_call(
        _tail_kernel,
        grid=(M // tm,),
        in_specs=[
            row,
            pl.BlockSpec((1, N_MEM, 2 * D_MODEL), lambda i: (i // tpb, 0, 0)),
            _const_spec(xa_wq.shape), _const_spec(xa_wo.shape),
            _const_spec(ln2_g.shape), _const_spec(ln2_b.shape),
            _const_spec(ffn_w_in.shape), _const_spec(ffn_w_out.shape),
            _const_spec(ln3_g.shape), _const_spec(ln3_b.shape),
        ],
        out_specs=row,
        out_shape=jax.ShapeDtypeStruct((M, D_MODEL), F32),
        compiler_params=_params("arbitrary"),
        name="xattn_ffn",
    )(x1, kv3, xa_wq, xa_wo, ln2_g, ln2_b, ffn_w_in, ffn_w_out, ln3_g, ln3_b)


def _rope_tables(T):
    inv = ROPE_THETA ** (-jnp.arange(0, ROT_DIM, 2, dtype=F32) / ROT_DIM)
    ang = jnp.arange(T, dtype=F32)[:, None] * inv[None, :]
    cos, sin = jnp.cos(ang), jnp.sin(ang)
    ones = jnp.ones((T, DA_HEAD_DIM - ROT_DIM), F32)
    zeros = jnp.zeros((T, DA_HEAD_DIM - ROT_DIM), F32)
    z8 = jnp.zeros((T, ROT_HALF), F32)
    cos64 = jnp.concatenate([cos, cos, ones], axis=1)
    sa64 = jnp.concatenate([z8, sin, zeros], axis=1)
    sb64 = jnp.concatenate([-sin, z8, zeros], axis=1)
    rep = LANES_V7X // DA_HEAD_DIM
    return (jnp.tile(cos64, (1, rep)), jnp.tile(sa64, (1, rep)), jnp.tile(sb64, (1, rep)))


def _trunk(x, mem, w):
    B, T, _ = x.shape
    M = B * T
    x2d = x.reshape(M, D_MODEL)
    cos_t, sa_t, sb_t = _rope_tables(T)
    qk, vt, xr, act = _proj(x2d, w["w_in"], cos_t, sa_t, sb_t, T)
    hf, hb = _lru(xr.reshape(B, T, D_MODEL), w["conv_w"], w["conv_b"], w["w_gates"],
                  w["b_gates"], w["lru_a"])
    attn = _attention(qk, vt, w["lam4"], w["subln_g"])
    x1 = _merge(attn.reshape(M, D_MODEL), hf.reshape(M, D_MODEL), hb.reshape(M, D_MODEL), act,
                x2d, w["p_attn"], w["p_lru"], w["w_mix_out"], w["ln1_g"], w["ln1_b"])
    kv = _memkv(mem.reshape(B * N_MEM, D_MODEL), w["xa_wkv"])
    y = _tail(x1, kv.reshape(B, N_MEM, 2 * D_MODEL), T, w["xa_wq"], w["xa_wo"], w["ln2_g"],
              w["ln2_b"], w["ffn_w_in"], w["ffn_w_out"], w["ln3_g"], w["ln3_b"])
    return y.reshape(B, T, D_MODEL)


def _prepare_weights(w_in, lambda_q1, lambda_k1, lambda_q2, lambda_k2, subln_g, conv_w, conv_b,
                     lru_wa, lru_ba, lru_wx, lru_bx, lru_a, p_attn, p_lru, w_mix_out, ln1_g, ln1_b,
                     xa_wq, xa_wkv, xa_wo, ln2_g, ln2_b, ffn_w_in, ffn_w_out, ln3_g, ln3_b):
    row = lambda v: v[0].reshape(1, -1).astype(F32)
    return {
        "w_in": w_in[0].astype(BF16),
        "lam4": jnp.stack([lambda_q1[0], lambda_k1[0], lambda_q2[0], lambda_k2[0]]).astype(F32),
        "subln_g": row(subln_g),
        "conv_w": conv_w[0].astype(F32),
        "conv_b": row(conv_b),
        "w_gates": jnp.concatenate([lru_wa[0], lru_wx[0]], axis=-1).astype(BF16),
        "b_gates": jnp.concatenate([lru_ba[0], lru_bx[0]], axis=-1).astype(F32),
        "lru_a": lru_a[0].astype(F32),
        "p_attn": p_attn[0].astype(BF16),
        "p_lru": p_lru[0].astype(BF16),
        "w_mix_out": w_mix_out[0].astype(BF16),
        "ln1_g": row(ln1_g), "ln1_b": row(ln1_b),
        "xa_wq": xa_wq[0].astype(BF16),
        "xa_wkv": xa_wkv[0].astype(BF16),
        "xa_wo": xa_wo[0].astype(BF16),
        "ln2_g": row(ln2_g), "ln2_b": row(ln2_b),
        "ffn_w_in": ffn_w_in[0].astype(BF16),
        "ffn_w_out": ffn_w_out[0].astype(BF16),
        "ln3_g": row(ln3_g), "ln3_b": row(ln3_b),
    }


def kernel(x_prompt, x_sample, mem_prompt, mem_sample, w_in, lambda_q1, lambda_k1, lambda_q2, lambda_k2, subln_g, conv_w, conv_b, lru_wa, lru_ba, lru_wx, lru_bx, lru_a, p_attn, p_lru, w_mix_out, ln1_g, ln1_b, xa_wq, xa_wkv, xa_wo, ln2_g, ln2_b, ffn_w_in, ffn_w_out, ln3_g, ln3_b):
    w = _prepare_weights(w_in, lambda_q1, lambda_k1, lambda_q2, lambda_k2, subln_g, conv_w, conv_b,
                         lru_wa, lru_ba, lru_wx, lru_bx, lru_a, p_attn, p_lru, w_mix_out, ln1_g,
                         ln1_b, xa_wq, xa_wkv, xa_wo, ln2_g, ln2_b, ffn_w_in, ffn_w_out, ln3_g, ln3_b)
    return (_trunk(x_prompt, mem_prompt, w), _trunk(x_sample, mem_sample, w))
```

```python
import functools
import math

import jax
import jax.numpy as jnp
from jax import lax
from jax.experimental import pallas as pl
from jax.experimental.pallas import tpu as pltpu

F32 = jnp.float32
BF16 = jnp.bfloat16

D_MODEL = 1024
N_MEM = 256
DA_HEADS = 8
DA_HEAD_DIM = 64
DA_PAIR = 2 * DA_HEAD_DIM
ROT_DIM = DA_HEAD_DIM // 4
ROT_HALF = ROT_DIM // 2
ROPE_THETA = 500000.0
SUBLN_EPS = 1e-5
LRU_BLOCKS = 8
LRU_BLOCK_DIM = D_MODEL // LRU_BLOCKS
CONV_WIDTH = 4
LRU_C = 8.0
N_SEG = 7
XA_HEADS = 4
XA_HEAD_DIM = D_MODEL // XA_HEADS
D_FF = ((8 * D_MODEL + 3 * 256 - 1) // (3 * 256)) * 256
DEPTH = 1
DEEPNORM_ALPHA = (2.0 * DEPTH) ** 0.25
LN_EPS = 1e-5
LAMBDA_INIT = 0.8 - 0.6 * math.exp(-0.3 * 0)

LANES_V7X = 128
SUBLANES_V7X = 8
VMEM_LIMIT_V7X = 56 * 1024 * 1024

PROJ_ROWS = 512
LRU_CHUNK = 512
ATTN_TQ = 512
ATTN_TK = 1024
ATTN_HEAD_TOKENS = 8192
ROW_TILE = 512
FFN_CHUNKS = ((0, 1024), (1024, 1024), (2048, D_FF - 2048))
NEG_BIG = -1e30
ONES_ROWS = 16


def _params(*sem):
    return pltpu.CompilerParams(dimension_semantics=sem, vmem_limit_bytes=VMEM_LIMIT_V7X)


def _const_spec(shape):
    nd = len(shape)
    return pl.BlockSpec(shape, lambda *_: (0,) * nd, pipeline_mode=pl.Buffered(1))


def _layer_norm(x, g, b):
    mu = jnp.mean(x, axis=-1, keepdims=True)
    xc = x - mu
    var = jnp.mean(xc * xc, axis=-1, keepdims=True)
    return xc * lax.rsqrt(var + LN_EPS) * g + b


def _proj_kernel(x_ref, w_ref, cos_ref, sa_ref, sb_ref, qk_ref, vt_ref, xr_ref, act_ref):
    xb = x_ref[...].astype(BF16)

    def matmul(seg, nseg):
        return jnp.dot(xb, w_ref[:, seg * D_MODEL:(seg + nseg) * D_MODEL],
                       preferred_element_type=F32)

    def cols(acc, i):
        return acc[:, i * D_MODEL:(i + 1) * D_MODEL]

    def rope_store(acc, scale, base):
        c, sa, sb = cos_ref[...], sa_ref[...], sb_ref[...]
        for h in range(DA_HEADS):
            blk = acc[:, h * DA_PAIR:(h + 1) * DA_PAIR]
            r = (blk * c + pltpu.roll(blk, ROT_HALF, 1) * sa
                 + pltpu.roll(blk, LANES_V7X - ROT_HALF, 1) * sb)
            qk_ref[base + h] = (r * scale).astype(BF16)

    act_ref[...] = matmul(4, 3).astype(BF16)
    qk = matmul(0, 2)
    rope_store(cols(qk, 0), DA_HEAD_DIM ** -0.5 * math.log2(math.e), 0)
    rope_store(cols(qk, 1), 1.0, DA_HEADS)
    vx = matmul(2, 2)
    ones = jnp.ones((ONES_ROWS, vx.shape[0]), BF16)
    for h in range(DA_HEADS):
        vt_ref[0, h, 0:DA_PAIR, :] = vx[:, h * DA_PAIR:(h + 1) * DA_PAIR].T.astype(BF16)
        vt_ref[0, h, DA_PAIR:DA_PAIR + ONES_ROWS, :] = ones
    xr_ref[...] = cols(vx, 1)


def _proj(x2d, w_in, cos_t, sa_t, sb_t, T):
    M = x2d.shape[0]
    tm = min(PROJ_ROWS, T)
    tpb = T // tm
    return pl.pallas_call(
        _proj_kernel,
        grid=(M // tm,),
        in_specs=[
            pl.BlockSpec((tm, D_MODEL), lambda i: (i, 0)),
            _const_spec(w_in.shape),
            pl.BlockSpec((tm, LANES_V7X), lambda i: (i % tpb, 0)),
            pl.BlockSpec((tm, LANES_V7X), lambda i: (i % tpb, 0)),
            pl.BlockSpec((tm, LANES_V7X), lambda i: (i % tpb, 0)),
        ],
        out_specs=[
            pl.BlockSpec((2 * DA_HEADS, tm, DA_PAIR), lambda i: (0, i, 0)),
            pl.BlockSpec((1, DA_HEADS, DA_PAIR + ONES_ROWS, tm),
                         lambda i: (i // tpb, 0, 0, i % tpb)),
            pl.BlockSpec((tm, D_MODEL), lambda i: (i, 0)),
            pl.BlockSpec((tm, 3 * D_MODEL), lambda i: (i, 0)),
        ],
        out_shape=[
            jax.ShapeDtypeStruct((2 * DA_HEADS, M, DA_PAIR), BF16),
            jax.ShapeDtypeStruct((M // T, DA_HEADS, DA_PAIR + ONES_ROWS, T), BF16),
            jax.ShapeDtypeStruct((M, D_MODEL), F32),
            jax.ShapeDtypeStruct((M, 3 * D_MODEL), BF16),
        ],
        compiler_params=_params("arbitrary"),
        name="proj",
    )(x2d, w_in, cos_t, sa_t, sb_t)


def _lru_kernel(xf_ref, xfp_ref, xfn_ref, xr_ref, xrp_ref, xrn_ref, cw_ref, cb_ref, w_ref,
                bias_ref, la_ref, hf_ref, hb_ref, af_ref, ab_ref, carry_ref, *, tc):
    c = pl.program_id(1)
    n_c = pl.num_programs(1)

    @pl.when(c == 0)
    def _():
        carry_ref[...] = jnp.zeros_like(carry_ref)

    n_ext = tc + 2 * SUBLANES_V7X

    def prepare(x_ref, xp_ref, xn_ref, d, has_prev, has_next, at_start, start_row, a_ref, u_ref):
        prev = jnp.where(has_prev, xp_ref[0], 0.0)
        nxt = jnp.where(has_next, xn_ref[0], 0.0)
        xall = jnp.concatenate([prev, x_ref[0], nxt], axis=0)
        conv = (cw_ref[0:1, :] * pltpu.roll(xall, 2, 0)
                + cw_ref[1:2, :] * pltpu.roll(xall, 1, 0)
                + cw_ref[2:3, :] * xall
                + cw_ref[3:4, :] * pltpu.roll(xall, n_ext - 1, 0))
        xc = conv[SUBLANES_V7X:SUBLANES_V7X + tc] + cb_ref[...]
        xcb = xc.astype(BF16)
        lam = la_ref[d:d + 1, :]
        decay = -LRU_C * (jnp.maximum(-lam, 0.0) + jnp.log1p(jnp.exp(-jnp.abs(lam))))
        for n in range(LRU_BLOCKS):
            sl = slice(n * LRU_BLOCK_DIM, (n + 1) * LRU_BLOCK_DIM)
            y = jnp.dot(xcb[:, sl], w_ref[d, n], preferred_element_type=F32)
            r = jax.nn.sigmoid(y[:, :LRU_BLOCK_DIM] + bias_ref[d:d + 1, sl])
            gate = jax.nn.sigmoid(
                y[:, LRU_BLOCK_DIM:]
                + bias_ref[d:d + 1, D_MODEL + n * LRU_BLOCK_DIM:D_MODEL + (n + 1) * LRU_BLOCK_DIM])
            a = jnp.exp(r * decay[:, sl])
            m2 = 1.0 - a * a
            mult = jnp.where(m2 > 0.0, m2 * lax.rsqrt(m2), 0.0)
            gx = gate * xc[:, sl]
            a_ref[:, sl] = a
            u_ref[0, :, sl] = mult * gx
            u_ref[0, start_row:start_row + 1, sl] = jnp.where(
                at_start, gx[start_row:start_row + 1], (mult * gx)[start_row:start_row + 1])

    first = c == 0
    last = c == n_c - 1
    prepare(xf_ref, xfp_ref, xfn_ref, 0, jnp.logical_not(first), jnp.logical_not(last),
            first, 0, af_ref, hf_ref)
    prepare(xr_ref, xrp_ref, xrn_ref, 1, jnp.logical_not(last), jnp.logical_not(first),
            first, tc - 1, ab_ref, hb_ref)

    hf, hb = carry_ref[0:1, :], carry_ref[1:2, :]
    for t in range(tc):
        hf = af_ref[t:t + 1, :] * hf + hf_ref[0, t:t + 1, :]
        hf_ref[0, t:t + 1, :] = hf
        tb = tc - 1 - t
        hb = ab_ref[tb:tb + 1, :] * hb + hb_ref[0, tb:tb + 1, :]
        hb_ref[0, tb:tb + 1, :] = hb
    carry_ref[0:1, :] = hf
    carry_ref[1:2, :] = hb


def _lru(xr3, conv_w, conv_b, w_gates, b_gates, lru_a):
    B, T, _ = xr3.shape
    tc = min(LRU_CHUNK, T)
    n_c = T // tc
    hb8 = tc // SUBLANES_V7X
    n8 = T // SUBLANES_V7X

    def cur(f):
        return pl.BlockSpec((1, tc, D_MODEL), lambda b, c: (b, f(c, n_c), 0))

    def prev(f):
        return pl.BlockSpec((1, SUBLANES_V7X, D_MODEL),
                            lambda b, c: (b, jnp.maximum(f(c, n_c) * hb8 - 1, 0), 0))

    def nxt(f):
        return pl.BlockSpec((1, SUBLANES_V7X, D_MODEL),
                            lambda b, c: (b, jnp.minimum((f(c, n_c) + 1) * hb8, n8 - 1), 0))

    fwd = lambda c, n: c
    bwd = lambda c, n: n - 1 - c
    return pl.pallas_call(
        functools.partial(_lru_kernel, tc=tc),
        grid=(B, n_c),
        in_specs=[
            cur(fwd), prev(fwd), nxt(fwd), cur(bwd), prev(bwd), nxt(bwd),
            _const_spec(conv_w.shape), _const_spec(conv_b.shape), _const_spec(w_gates.shape),
            _const_spec(b_gates.shape), _const_spec(lru_a.shape),
        ],
        out_specs=[
            pl.BlockSpec((1, tc, D_MODEL), lambda b, c: (b, c, 0)),
            pl.BlockSpec((1, tc, D_MODEL), lambda b, c: (b, n_c - 1 - c, 0)),
        ],
        out_shape=[jax.ShapeDtypeStruct((B, T, D_MODEL), F32)] * 2,
        scratch_shapes=[
            pltpu.VMEM((tc, D_MODEL), F32),
            pltpu.VMEM((tc, D_MODEL), F32),
            pltpu.VMEM((2, D_MODEL), F32),
        ],
        compiler_params=_params("arbitrary", "arbitrary"),
        name="lru",
    )(xr3, xr3, xr3, xr3, xr3, xr3, conv_w, conv_b, w_gates, b_gates, lru_a)


def _attn_kernel(lam_ref, g_ref, q_ref, k_ref, vt_ref, o_ref, qt_ref, st_ref, mc_ref, acc_ref,
                 m_ref, *, heads, n_q, n_kv, tq, tk):
    n_blk = heads * n_q

    def split(bi):
        if heads == 1:
            return 0, bi
        return lax.div(bi, n_q), lax.rem(bi, n_q)

    def prep_q(bi, slot):
        hh, qi = split(bi)
        off = pl.multiple_of(qi * tq, tq)
        qt = q_ref[hh, pl.ds(off, tq), :].astype(F32).T
        row = lax.broadcasted_iota(jnp.int32, qt.shape, 0)
        qt_ref[slot] = jnp.concatenate([jnp.where(row < DA_HEAD_DIM, qt, 0.0),
                                        jnp.where(row >= DA_HEAD_DIM, qt, 0.0)], axis=1).astype(BF16)

    def scores(qslot, hh, c, slot):
        off = pl.multiple_of(c * tk, tk)
        st = jnp.dot(k_ref[hh, pl.ds(off, tk), :], qt_ref[qslot], preferred_element_type=F32)
        st_ref[slot] = st
        mc_ref[slot] = jnp.max(st, axis=0, keepdims=True)

    def accumulate(aslot, hh, c, slot):
        off = pl.multiple_of(c * tk, tk)
        st = st_ref[slot]
        m_old = m_ref[...]
        m_new = jnp.maximum(m_old, mc_ref[slot])
        alpha = jnp.exp2(m_old - m_new)
        pt = jnp.exp2(st - m_new).astype(BF16)
        acc_ref[aslot] = alpha * acc_ref[aslot] + jnp.dot(vt_ref[0, hh, :, pl.ds(off, tk)], pt,
                                                         preferred_element_type=F32)
        m_ref[...] = m_new

    lam = (jnp.exp(jnp.sum(lam_ref[0:1, :] * lam_ref[1:2, :], axis=-1, keepdims=True))
           - jnp.exp(jnp.sum(lam_ref[2:3, :] * lam_ref[3:4, :], axis=-1, keepdims=True))
           + LAMBDA_INIT)

    def finalize(bi, aslot):
        hh, qi = split(bi)
        ot = (acc_ref[aslot, 0:DA_PAIR, 0:tq] / acc_ref[aslot, DA_PAIR:DA_PAIR + 1, 0:tq]
              - lam * (acc_ref[aslot, 0:DA_PAIR, tq:2 * tq]
                       / acc_ref[aslot, DA_PAIR:DA_PAIR + 1, tq:2 * tq]))
        o = ot.T
        o = o * lax.rsqrt(jnp.mean(o * o, axis=-1, keepdims=True) + SUBLN_EPS) * g_ref[...]
        off = pl.multiple_of(qi * tq, tq)
        col = hh * DA_PAIR if heads == 1 else pl.multiple_of(hh * DA_PAIR, DA_PAIR)
        o_ref[0, pl.ds(off, tq), pl.ds(col, DA_PAIR)] = (o * (1.0 - LAMBDA_INIT)).astype(BF16)

    prep_q(0, 0)
    scores(0, 0, 0, 0)
    acc_ref[1] = jnp.ones(acc_ref.shape[1:], F32)

    def q_block(bi, _):
        qs = lax.rem(bi, 2)
        hh, _ = split(bi)
        nxt = jnp.minimum(bi + 1, n_blk - 1)
        hh_nxt, _ = split(nxt)
        acc_ref[qs] = jnp.zeros(acc_ref.shape[1:], F32)
        m_ref[...] = jnp.full(m_ref.shape, NEG_BIG, F32)

        def pair(j, _):
            scores(qs, hh, 2 * j + 1, 1)
            accumulate(qs, hh, 2 * j, 0)
            scores(qs, hh, 2 * j + 2, 0)
            accumulate(qs, hh, 2 * j + 1, 1)
            return 0

        lax.fori_loop(0, n_kv // 2 - 1, pair, 0)
        scores(qs, hh, n_kv - 1, 1)
        accumulate(qs, hh, n_kv - 2, 0)
        finalize(jnp.maximum(bi - 1, 0), 1 - qs)
        prep_q(nxt, 1 - qs)
        scores(1 - qs, hh_nxt, 0, 0)
        accumulate(qs, hh, n_kv - 1, 1)
        return 0

    lax.fori_loop(0, n_blk, q_block, 0)
    finalize(n_blk - 1, (n_blk - 1) % 2)


def _attention(qk, vt, lam4, subln_g):
    B, _, _, T = vt.shape
    tq = min(ATTN_TQ, T)
    tk = min(ATTN_TK, T)
    assert T % (2 * tk) == 0 and T % tq == 0, "key chunks are consumed in pairs"
    heads = max(1, min(DA_HEADS, ATTN_HEAD_TOKENS // T))
    groups = DA_HEADS // heads
    return pl.pallas_call(
        functools.partial(_attn_kernel, heads=heads, n_q=T // tq, n_kv=T // tk, tq=tq, tk=tk),
        grid=(B, groups),
        in_specs=[
            _const_spec(lam4.shape), _const_spec(subln_g.shape),
            pl.BlockSpec((heads, T, DA_PAIR), lambda b, h: (h, b, 0)),
            pl.BlockSpec((heads, T, DA_PAIR), lambda b, h: (groups + h, b, 0)),
            pl.BlockSpec((1, heads, DA_PAIR + ONES_ROWS, T), lambda b, h: (b, h, 0, 0)),
        ],
        out_specs=pl.BlockSpec((1, T, heads * DA_PAIR), lambda b, h: (b, 0, h)),
        out_shape=jax.ShapeDtypeStruct((B, T, D_MODEL), BF16),
        scratch_shapes=[
            pltpu.VMEM((2, DA_PAIR, 2 * tq), BF16),
            pltpu.VMEM((2, tk, 2 * tq), F32),
            pltpu.VMEM((2, 1, 2 * tq), F32),
            pltpu.VMEM((2, DA_PAIR + ONES_ROWS, 2 * tq), F32),
            pltpu.VMEM((1, 2 * tq), F32),
        ],
        compiler_params=_params("arbitrary", "arbitrary"),
        name="diff_attn",
    )(lam4, subln_g, qk, qk, vt)


def _merge_kernel(attn_ref, hf_ref, hb_ref, gy_ref, sa_ref, sl_ref, x_ref, pa_ref, plru_ref,
                  wo_ref, g_ref, b_ref, o_ref):
    a_proj = jnp.dot(attn_ref[...], pa_ref[...], preferred_element_type=F32)
    lru_out = ((hf_ref[...] + hb_ref[...]) * jax.nn.gelu(gy_ref[...].astype(F32))).astype(BF16)
    l_proj = jnp.dot(lru_out, plru_ref[...], preferred_element_type=F32)
    merged = (jax.nn.sigmoid(sa_ref[...].astype(F32)) * a_proj
              + jax.nn.sigmoid(sl_ref[...].astype(F32)) * l_proj)
    m = jnp.dot(merged.astype(BF16), wo_ref[...], preferred_element_type=F32)
    o_ref[...] = _layer_norm(DEEPNORM_ALPHA * x_ref[...] + m, g_ref[...], b_ref[...])


def _merge(attn2, hf2, hb2, act2, x2d, p_attn, p_lru, w_mix_out, ln_g, ln_b):
    M = x2d.shape[0]
    tm = ROW_TILE
    row = lambda seg: pl.BlockSpec((tm, D_MODEL), lambda i: (i, seg))
    return pl.pallas_call(
        _merge_kernel,
        grid=(M // tm,),
        in_specs=[
            row(0), row(0), row(0), row(0), row(1), row(2), row(0),
            _const_spec(p_attn.shape), _const_spec(p_lru.shape), _const_spec(w_mix_out.shape),
            _const_spec(ln_g.shape), _const_spec(ln_b.shape),
        ],
        out_specs=row(0),
        out_shape=jax.ShapeDtypeStruct((M, D_MODEL), F32),
        compiler_params=_params("arbitrary"),
        name="merge_ln1",
    )(attn2, hf2, hb2, act2, act2, act2, x2d, p_attn, p_lru, w_mix_out, ln_g, ln_b)


def _memkv_kernel(m_ref, w_ref, o_ref):
    o_ref[...] = jnp.dot(m_ref[...].astype(BF16), w_ref[...],
                         preferred_element_type=F32).astype(BF16)


def _memkv(mem2, xa_wkv):
    M = mem2.shape[0]
    tm = min(ROW_TILE, M)
    return pl.pallas_call(
        _memkv_kernel,
        grid=(M // tm, 2),
        in_specs=[
            pl.BlockSpec((tm, D_MODEL), lambda i, j: (i, 0)),
            pl.BlockSpec((D_MODEL, D_MODEL), lambda i, j: (0, j)),
        ],
        out_specs=pl.BlockSpec((tm, D_MODEL), lambda i, j: (i, j)),
        out_shape=jax.ShapeDtypeStruct((M, 2 * D_MODEL), BF16),
        compiler_params=_params("arbitrary", "arbitrary"),
        name="mem_kv",
    )(mem2, xa_wkv)


def _tail_kernel(x_ref, kv_ref, wq_ref, wo_ref, g2_ref, b2_ref, wi_ref, wf_ref, g3_ref, b3_ref,
                 o_ref):
    x1 = x_ref[...]
    q = (jnp.dot(x1.astype(BF16), wq_ref[...], preferred_element_type=F32)
         * (XA_HEAD_DIM ** -0.5)).astype(BF16)
    heads = []
    for h in range(XA_HEADS):
        sl = slice(h * XA_HEAD_DIM, (h + 1) * XA_HEAD_DIM)
        kh = kv_ref[0, :, sl]
        vh = kv_ref[0, :, D_MODEL + h * XA_HEAD_DIM:D_MODEL + (h + 1) * XA_HEAD_DIM]
        s = lax.dot_general(q[:, sl], kh, (((1,), (1,)), ((), ())), preferred_element_type=F32)
        e = jnp.exp(s - jnp.max(s, axis=-1, keepdims=True))
        p = e / jnp.sum(e, axis=-1, keepdims=True)
        heads.append(jnp.dot(p.astype(BF16), vh, preferred_element_type=F32).astype(BF16))
    xa = jnp.dot(jnp.concatenate(heads, axis=-1), wo_ref[...], preferred_element_type=F32)
    x2 = _layer_norm(DEEPNORM_ALPHA * x1 + xa, g2_ref[...], b2_ref[...])
    x2b = x2.astype(BF16)
    y = None
    for c0, cw in FFN_CHUNKS:
        g = jnp.dot(x2b, wi_ref[:, c0:c0 + cw], preferred_element_type=F32)
        u = jnp.dot(x2b, wi_ref[:, D_FF + c0:D_FF + c0 + cw], preferred_element_type=F32)
        hcb = (g * jax.nn.sigmoid(g) * u).astype(BF16)
        part = jnp.dot(hcb, wf_ref[c0:c0 + cw, :], preferred_element_type=F32)
        y = part if y is None else y + part
    o_ref[...] = _layer_norm(DEEPNORM_ALPHA * x2 + y, g3_ref[...], b3_ref[...])


def _tail(x1, kv3, T, xa_wq, xa_wo, ln2_g, ln2_b, ffn_w_in, ffn_w_out, ln3_g, ln3_b):
    M = x1.shape[0]
    tm = ROW_TILE
    tpb = T // tm
    row = pl.BlockSpec((tm, D_MODEL), lambda i: (i, 0))
    return pl.pallas_call(
        _tail_kernel,
        grid=(M // tm,),
        in_specs=[
            row,
            pl.BlockSpec((1, N_MEM, 2 * D_MODEL), lambda i: (i // tpb, 0, 0)),
            _const_spec(xa_wq.shape), _const_spec(xa_wo.shape),
            _const_spec(ln2_g.shape), _const_spec(ln2_b.shape),
            _const_spec(ffn_w_in.shape), _const_spec(ffn_w_out.shape),
            _const_spec(ln3_g.shape), _const_spec(ln3_b.shape),
        ],
        out_specs=row,
        out_shape=jax.ShapeDtypeStruct((M, D_MODEL), F32),
        compiler_params=_params("arbitrary"),
        name="xattn_ffn",
    )(x1, kv3, xa_wq, xa_wo, ln2_g, ln2_b, ffn_w_in, ffn_w_out, ln3_g, ln3_b)


def _rope_tables(T):
    inv = ROPE_THETA ** (-jnp.arange(0, ROT_DIM, 2, dtype=F32) / ROT_DIM)
    ang = jnp.arange(T, dtype=F32)[:, None] * inv[None, :]
    cos, sin = jnp.cos(ang), jnp.sin(ang)
    ones = jnp.ones((T, DA_HEAD_DIM - ROT_DIM), F32)
    zeros = jnp.zeros((T, DA_HEAD_DIM - ROT_DIM), F32)
    z8 = jnp.zeros((T, ROT_HALF), F32)
    cos64 = jnp.concatenate([cos, cos, ones], axis=1)
    sa64 = jnp.concatenate([z8, sin, zeros], axis=1)
    sb64 = jnp.concatenate([-sin, z8, zeros], axis=1)
    rep = LANES_V7X // DA_HEAD_DIM
    return (jnp.tile(cos64, (1, rep)), jnp.tile(sa64, (1, rep)), jnp.tile(sb64, (1, rep)))


def _trunk(x, mem, w):
    B, T, _ = x.shape
    M = B * T
    x2d = x.reshape(M, D_MODEL)
    cos_t, sa_t, sb_t = _rope_tables(T)
    qk, vt, xr, act = _proj(x2d, w["w_in"], cos_t, sa_t, sb_t, T)
    hf, hb = _lru(xr.reshape(B, T, D_MODEL), w["conv_w"], w["conv_b"], w["w_gates"],
                  w["b_gates"], w["lru_a"])
    attn = _attention(qk, vt, w["lam4"], w["subln_g"])
    x1 = _merge(attn.reshape(M, D_MODEL), hf.reshape(M, D_MODEL), hb.reshape(M, D_MODEL), act,
                x2d, w["p_attn"], w["p_lru"], w["w_mix_out"], w["ln1_g"], w["ln1_b"])
    kv = _memkv(mem.reshape(B * N_MEM, D_MODEL), w["xa_wkv"])
    y = _tail(x1, kv.reshape(B, N_MEM, 2 * D_MODEL), T, w["xa_wq"], w["xa_wo"], w["ln2_g"],
              w["ln2_b"], w["ffn_w_in"], w["ffn_w_out"], w["ln3_g"], w["ln3_b"])
    return y.reshape(B, T, D_MODEL)


def _prepare_weights(w_in, lambda_q1, lambda_k1, lambda_q2, lambda_k2, subln_g, conv_w, conv_b,
                     lru_wa, lru_ba, lru_wx, lru_bx, lru_a, p_attn, p_lru, w_mix_out, ln1_g, ln1_b,
                     xa_wq, xa_wkv, xa_wo, ln2_g, ln2_b, ffn_w_in, ffn_w_out, ln3_g, ln3_b):
    row = lambda v: v[0].reshape(1, -1).astype(F32)
    return {
        "w_in": w_in[0].astype(BF16),
        "lam4": jnp.stack([lambda_q1[0], lambda_k1[0], lambda_q2[0], lambda_k2[0]]).astype(F32),
        "subln_g": row(subln_g),
        "conv_w": conv_w[0].astype(F32),
        "conv_b": row(conv_b),
        "w_gates": jnp.concatenate([lru_wa[0], lru_wx[0]], axis=-1).astype(BF16),
        "b_gates": jnp.concatenate([lru_ba[0], lru_bx[0]], axis=-1).astype(F32),
        "lru_a": lru_a[0].astype(F32),
        "p_attn": p_attn[0].astype(BF16),
        "p_lru": p_lru[0].astype(BF16),
        "w_mix_out": w_mix_out[0].astype(BF16),
        "ln1_g": row(ln1_g), "ln1_b": row(ln1_b),
        "xa_wq": xa_wq[0].astype(BF16),
        "xa_wkv": xa_wkv[0].astype(BF16),
        "xa_wo": xa_wo[0].astype(BF16),
        "ln2_g": row(ln2_g), "ln2_b": row(ln2_b),
        "ffn_w_in": ffn_w_in[0].astype(BF16),
        "ffn_w_out": ffn_w_out[0].astype(BF16),
        "ln3_g": row(ln3_g), "ln3_b": row(ln3_b),
    }


def kernel(x_prompt, x_sample, mem_prompt, mem_sample, w_in, lambda_q1, lambda_k1, lambda_q2, lambda_k2, subln_g, conv_w, conv_b, lru_wa, lru_ba, lru_wx, lru_bx, lru_a, p_attn, p_lru, w_mix_out, ln1_g, ln1_b, xa_wq, xa_wkv, xa_wo, ln2_g, ln2_b, ffn_w_in, ffn_w_out, ln3_g, ln3_b):
    w = _prepare_weights(w_in, lambda_q1, lambda_k1, lambda_q2, lambda_k2, subln_g, conv_w, conv_b,
                         lru_wa, lru_ba, lru_wx, lru_bx, lru_a, p_attn, p_lru, w_mix_out, ln1_g,
                         ln1_b, xa_wq, xa_wkv, xa_wo, ln2_g, ln2_b, ffn_w_in, ffn_w_out, ln3_g, ln3_b)
    return (_trunk(x_prompt, mem_prompt, w), _trunk(x_sample, mem_sample, w))
```

```python
import functools
import math

import jax
import jax.numpy as jnp
from jax import lax
from jax.experimental import pallas as pl
from jax.experimental.pallas import tpu as pltpu

F32 = jnp.float32
BF16 = jnp.bfloat16

D_MODEL = 1024
N_MEM = 256
DA_HEADS = 8
DA_HEAD_DIM = 64
DA_PAIR = 2 * DA_HEAD_DIM
ROT_DIM = DA_HEAD_DIM // 4
ROT_HALF = ROT_DIM // 2
ROPE_THETA = 500000.0
SUBLN_EPS = 1e-5
LRU_BLOCKS = 8
LRU_BLOCK_DIM = D_MODEL // LRU_BLOCKS
CONV_WIDTH = 4
LRU_C = 8.0
N_SEG = 7
XA_HEADS = 4
XA_HEAD_DIM = D_MODEL // XA_HEADS
D_FF = ((8 * D_MODEL + 3 * 256 - 1) // (3 * 256)) * 256
DEPTH = 1
DEEPNORM_ALPHA = (2.0 * DEPTH) ** 0.25
LN_EPS = 1e-5
LAMBDA_INIT = 0.8 - 0.6 * math.exp(-0.3 * 0)

LANES_V7X = 128
SUBLANES_V7X = 8
VMEM_LIMIT_V7X = 56 * 1024 * 1024

PROJ_ROWS = 512
LRU_CHUNK = 512
ATTN_TQ = 512
ATTN_TK = 1024
ATTN_HEAD_TOKENS = 8192
ROW_TILE = 512
FFN_CHUNKS = ((0, 1024), (1024, 1024), (2048, D_FF - 2048))
NEG_BIG = -1e30
ONES_ROWS = 16


def _params(*sem):
    return pltpu.CompilerParams(dimension_semantics=sem, vmem_limit_bytes=VMEM_LIMIT_V7X)


def _const_spec(shape):
    nd = len(shape)
    return pl.BlockSpec(shape, lambda *_: (0,) * nd, pipeline_mode=pl.Buffered(1))


def _sigmoid(z):
    return 0.5 * jnp.tanh(0.5 * z) + 0.5


def _layer_norm(x, g, b):
    mu = jnp.mean(x, axis=-1, keepdims=True)
    xc = x - mu
    var = jnp.mean(xc * xc, axis=-1, keepdims=True)
    return xc * lax.rsqrt(var + LN_EPS) * g + b


def _proj_kernel(x_ref, xp_ref, xn_ref, w_ref, cos_ref, sa_ref, sb_ref, cw_ref, cb_ref,
                 qk_ref, vt_ref, xc_ref, act_ref, *, tpb):
    i = pl.program_id(0)
    tm = x_ref.shape[0]
    s8 = SUBLANES_V7X
    ti = lax.rem(i, tpb)
    prev = jnp.where(ti > 0, xp_ref[...], 0.0)
    nxt = jnp.where(ti < tpb - 1, xn_ref[...], 0.0)
    xb_ext = jnp.concatenate([prev, x_ref[...], nxt], axis=0).astype(BF16)
    xb = xb_ext[s8:s8 + tm]

    def matmul(lhs, seg, nseg):
        return jnp.dot(lhs, w_ref[:, seg * D_MODEL:(seg + nseg) * D_MODEL],
                       preferred_element_type=F32)

    def cols(acc, j):
        return acc[:, j * D_MODEL:(j + 1) * D_MODEL]

    def rope_store(acc, scale, base):
        c, sa, sb = cos_ref[...], sa_ref[...], sb_ref[...]
        for h in range(DA_HEADS):
            blk = acc[:, h * DA_PAIR:(h + 1) * DA_PAIR]
            r = (blk * c + pltpu.roll(blk, ROT_HALF, 1) * sa
                 + pltpu.roll(blk, LANES_V7X - ROT_HALF, 1) * sb)
            qk_ref[base + h] = (r * scale).astype(BF16)

    xr = matmul(xb_ext, 3, 1)
    n_ext = tm + 2 * s8

    def conv_store(lo, hi):
        sl = slice(lo * LANES_V7X, hi * LANES_V7X)
        xs = xr[:, sl]
        conv = (cw_ref[0:1, sl] * pltpu.roll(xs, 2, 0)
                + cw_ref[1:2, sl] * pltpu.roll(xs, 1, 0)
                + cw_ref[2:3, sl] * xs
                + cw_ref[3:4, sl] * pltpu.roll(xs, n_ext - 1, 0))
        xc_ref[:, sl] = conv[s8:s8 + tm] + cb_ref[:, sl]

    for seg in range(3):
        act_ref[:, seg * D_MODEL:(seg + 1) * D_MODEL] = matmul(xb, 4 + seg, 1).astype(BF16)
        conv_store(2 * seg, 2 * seg + 2)
    rope_store(matmul(xb, 0, 1), DA_HEAD_DIM ** -0.5 * math.log2(math.e), 0)
    conv_store(6, 8)
    rope_store(matmul(xb, 1, 1), 1.0, DA_HEADS)
    v = matmul(xb, 2, 1)
    ones = jnp.ones((ONES_ROWS, tm), BF16)
    for h in range(DA_HEADS):
        vt_ref[0, h, 0:DA_PAIR, :] = v[:, h * DA_PAIR:(h + 1) * DA_PAIR].T.astype(BF16)
        vt_ref[0, h, DA_PAIR:DA_PAIR + ONES_ROWS, :] = ones


def _proj(x2d, w_in, cos_t, sa_t, sb_t, conv_w, conv_b, T):
    M = x2d.shape[0]
    tm = min(PROJ_ROWS, T)
    tpb = T // tm
    t8 = tm // SUBLANES_V7X
    n8 = M // SUBLANES_V7X
    return pl.pallas_call(
        functools.partial(_proj_kernel, tpb=tpb),
        grid=(M // tm,),
        in_specs=[
            pl.BlockSpec((tm, D_MODEL), lambda i: (i, 0)),
            pl.BlockSpec((SUBLANES_V7X, D_MODEL), lambda i: (jnp.maximum(i * t8 - 1, 0), 0)),
            pl.BlockSpec((SUBLANES_V7X, D_MODEL), lambda i: (jnp.minimum((i + 1) * t8, n8 - 1), 0)),
            _const_spec(w_in.shape),
            pl.BlockSpec((tm, LANES_V7X), lambda i: (i % tpb, 0)),
            pl.BlockSpec((tm, LANES_V7X), lambda i: (i % tpb, 0)),
            pl.BlockSpec((tm, LANES_V7X), lambda i: (i % tpb, 0)),
            _const_spec(conv_w.shape), _const_spec(conv_b.shape),
        ],
        out_specs=[
            pl.BlockSpec((2 * DA_HEADS, tm, DA_PAIR), lambda i: (0, i, 0)),
            pl.BlockSpec((1, DA_HEADS, DA_PAIR + ONES_ROWS, tm),
                         lambda i: (i // tpb, 0, 0, i % tpb)),
            pl.BlockSpec((tm, D_MODEL), lambda i: (i, 0)),
            pl.BlockSpec((tm, 3 * D_MODEL), lambda i: (i, 0)),
        ],
        out_shape=[
            jax.ShapeDtypeStruct((2 * DA_HEADS, M, DA_PAIR), BF16),
            jax.ShapeDtypeStruct((M // T, DA_HEADS, DA_PAIR + ONES_ROWS, T), BF16),
            jax.ShapeDtypeStruct((M, D_MODEL), F32),
            jax.ShapeDtypeStruct((M, 3 * D_MODEL), BF16),
        ],
        compiler_params=_params("arbitrary"),
        name="proj",
    )(x2d, x2d, x2d, w_in, cos_t, sa_t, sb_t, conv_w, conv_b)


def _lru_kernel(xf_ref, xr_ref, w_ref, bias_ref, la_ref, hf_ref, hb_ref, a0_ref, u0_ref, a1_ref, u1_ref,
                carry_ref, *, tc, n_c):
    s = pl.program_id(0)

    @pl.when(s == 0)
    def _():
        a1_ref[...] = jnp.zeros_like(a1_ref)
        u1_ref[...] = jnp.zeros_like(u1_ref)
        carry_ref[...] = jnp.zeros_like(carry_ref)

    def prepare(x_ref, d, at_start, start_row, a_ref, u_ref):
        xc = x_ref[0]
        xcb = xc.astype(BF16)
        lam = la_ref[d:d + 1, :]
        decay = -LRU_C * (jnp.maximum(-lam, 0.0) + jnp.log1p(jnp.exp(-jnp.abs(lam))))
        for n in range(LRU_BLOCKS):
            sl = slice(n * LRU_BLOCK_DIM, (n + 1) * LRU_BLOCK_DIM)
            y = jnp.dot(xcb[:, sl], w_ref[d, n], preferred_element_type=F32)
            r = _sigmoid(y[:, :LRU_BLOCK_DIM] + bias_ref[d:d + 1, sl])
            gate = _sigmoid(
                y[:, LRU_BLOCK_DIM:]
                + bias_ref[d:d + 1, D_MODEL + n * LRU_BLOCK_DIM:D_MODEL + (n + 1) * LRU_BLOCK_DIM])
            a = jnp.exp(r * decay[:, sl])
            m2 = 1.0 - a * a
            mult = jnp.where(m2 > 0.0, m2 * lax.rsqrt(m2), 0.0)
            gx = gate * xc[:, sl]
            a_ref[d, :, sl] = a
            u_ref[d, :, sl] = mult * gx
            u_ref[d, start_row:start_row + 1, sl] = jnp.where(
                at_start, gx[start_row:start_row + 1], (mult * gx)[start_row:start_row + 1])

    def step(fill_a, fill_u, scan_a, scan_u):
        at_start = lax.rem(s, n_c) == 0
        prepare(xf_ref, 0, at_start, 0, fill_a, fill_u)
        prepare(xr_ref, 1, at_start, tc - 1, fill_a, fill_u)
        fresh = lax.rem(s - 1, n_c) == 0
        hf = jnp.where(fresh, 0.0, carry_ref[0:1, :])
        hb = jnp.where(fresh, 0.0, carry_ref[1:2, :])
        for t in range(tc):
            hf = scan_a[0, t:t + 1, :] * hf + scan_u[0, t:t + 1, :]
            hf_ref[0, t:t + 1, :] = hf
            tb = tc - 1 - t
            hb = scan_a[1, tb:tb + 1, :] * hb + scan_u[1, tb:tb + 1, :]
            hb_ref[0, tb:tb + 1, :] = hb
        carry_ref[0:1, :] = hf
        carry_ref[1:2, :] = hb

    @pl.when(lax.rem(s, 2) == 0)
    def _():
        step(a0_ref, u0_ref, a1_ref, u1_ref)

    @pl.when(lax.rem(s, 2) == 1)
    def _():
        step(a1_ref, u1_ref, a0_ref, u0_ref)


def _lru(xc3, w_gates, b_gates, lru_a):
    B, T, _ = xc3.shape
    tc = min(LRU_CHUNK, T)
    n_c = T // tc
    n_steps = B * n_c + 1
    last = B * n_c - 1

    def chunk(g, reverse):
        g = jnp.clip(g, 0, last)
        c = g % n_c
        return g // n_c, (n_c - 1 - c) if reverse else c, 0

    return pl.pallas_call(
        functools.partial(_lru_kernel, tc=tc, n_c=n_c),
        grid=(n_steps,),
        in_specs=[
            pl.BlockSpec((1, tc, D_MODEL), lambda s: chunk(s, False)),
            pl.BlockSpec((1, tc, D_MODEL), lambda s: chunk(s, True)),
            _const_spec(w_gates.shape), _const_spec(b_gates.shape), _const_spec(lru_a.shape),
        ],
        out_specs=[
            pl.BlockSpec((1, tc, D_MODEL), lambda s: chunk(s - 1, False)),
            pl.BlockSpec((1, tc, D_MODEL), lambda s: chunk(s - 1, True)),
        ],
        out_shape=[jax.ShapeDtypeStruct((B, T, D_MODEL), F32)] * 2,
        scratch_shapes=[
            pltpu.VMEM((2, tc, D_MODEL), F32),
            pltpu.VMEM((2, tc, D_MODEL), F32),
            pltpu.VMEM((2, tc, D_MODEL), F32),
            pltpu.VMEM((2, tc, D_MODEL), F32),
            pltpu.VMEM((2, D_MODEL), F32),
        ],
        compiler_params=_params("arbitrary"),
        name="lru",
    )(xc3, xc3, w_gates, b_gates, lru_a)


def _attn_kernel(lam_ref, g_ref, q_ref, k_ref, vt_ref, o_ref, qt_ref, st_ref, mc_ref, acc_ref,
                 m_ref, *, heads, n_q, n_kv, tq, tk):
    n_blk = heads * n_q

    def split(bi):
        if heads == 1:
            return 0, bi
        return lax.div(bi, n_q), lax.rem(bi, n_q)

    def prep_q(bi, slot):
        hh, qi = split(bi)
        off = pl.multiple_of(qi * tq, tq)
        qt = q_ref[hh, pl.ds(off, tq), :].astype(F32).T
        row = lax.broadcasted_iota(jnp.int32, qt.shape, 0)
        qt_ref[slot] = jnp.concatenate([jnp.where(row < DA_HEAD_DIM, qt, 0.0),
                                        jnp.where(row >= DA_HEAD_DIM, qt, 0.0)], axis=1).astype(BF16)

    def scores(qslot, hh, c, slot):
        off = pl.multiple_of(c * tk, tk)
        st = jnp.dot(k_ref[hh, pl.ds(off, tk), :], qt_ref[qslot], preferred_element_type=F32)
        st_ref[slot] = st
        mc_ref[slot] = jnp.max(st, axis=0, keepdims=True)

    def accumulate(aslot, hh, c, slot):
        off = pl.multiple_of(c * tk, tk)
        st = st_ref[slot]
        m_old = m_ref[...]
        m_new = jnp.maximum(m_old, mc_ref[slot])
        alpha = jnp.exp2(m_old - m_new)
        pt = jnp.exp2(st - m_new).astype(BF16)
        acc_ref[aslot] = alpha * acc_ref[aslot] + jnp.dot(vt_ref[0, hh, :, pl.ds(off, tk)], pt,
                                                         preferred_element_type=F32)
        m_ref[...] = m_new

    lam = (jnp.exp(jnp.sum(lam_ref[0:1, :] * lam_ref[1:2, :], axis=-1, keepdims=True))
           - jnp.exp(jnp.sum(lam_ref[2:3, :] * lam_ref[3:4, :], axis=-1, keepdims=True))
           + LAMBDA_INIT)

    def finalize(bi, aslot):
        hh, qi = split(bi)
        ot = (acc_ref[aslot, 0:DA_PAIR, 0:tq] / acc_ref[aslot, DA_PAIR:DA_PAIR + 1, 0:tq]
              - lam * (acc_ref[aslot, 0:DA_PAIR, tq:2 * tq]
                       / acc_ref[aslot, DA_PAIR:DA_PAIR + 1, tq:2 * tq]))
        o = ot.T
        o = o * lax.rsqrt(jnp.mean(o * o, axis=-1, keepdims=True) + SUBLN_EPS) * g_ref[...]
        off = pl.multiple_of(qi * tq, tq)
        col = hh * DA_PAIR if heads == 1 else pl.multiple_of(hh * DA_PAIR, DA_PAIR)
        o_ref[0, pl.ds(off, tq), pl.ds(col, DA_PAIR)] = (o * (1.0 - LAMBDA_INIT)).astype(BF16)

    prep_q(0, 0)
    scores(0, 0, 0, 0)
    acc_ref[1] = jnp.ones(acc_ref.shape[1:], F32)

    def q_block(bi, _):
        qs = lax.rem(bi, 2)
        hh, _ = split(bi)
        nxt = jnp.minimum(bi + 1, n_blk - 1)
        hh_nxt, _ = split(nxt)
        acc_ref[qs] = jnp.zeros(acc_ref.shape[1:], F32)
        m_ref[...] = jnp.full(m_ref.shape, NEG_BIG, F32)

        def pair(j, _):
            scores(qs, hh, 2 * j + 1, 1)
            accumulate(qs, hh, 2 * j, 0)
            scores(qs, hh, 2 * j + 2, 0)
            accumulate(qs, hh, 2 * j + 1, 1)
            return 0

        lax.fori_loop(0, n_kv // 2 - 1, pair, 0)
        scores(qs, hh, n_kv - 1, 1)
        accumulate(qs, hh, n_kv - 2, 0)
        finalize(jnp.maximum(bi - 1, 0), 1 - qs)
        prep_q(nxt, 1 - qs)
        scores(1 - qs, hh_nxt, 0, 0)
        accumulate(qs, hh, n_kv - 1, 1)
        return 0

    lax.fori_loop(0, n_blk, q_block, 0)
    finalize(n_blk - 1, (n_blk - 1) % 2)


def _attention(qk, vt, lam4, subln_g):
    B, _, _, T = vt.shape
    tq = min(ATTN_TQ, T)
    tk = min(ATTN_TK, T)
    assert T % (2 * tk) == 0 and T % tq == 0, "key chunks are consumed in pairs"
    heads = max(1, min(DA_HEADS, ATTN_HEAD_TOKENS // T))
    groups = DA_HEADS // heads
    return pl.pallas_call(
        functools.partial(_attn_kernel, heads=heads, n_q=T // tq, n_kv=T // tk, tq=tq, tk=tk),
        grid=(B, groups),
        in_specs=[
            _const_spec(lam4.shape), _const_spec(subln_g.shape),
            pl.BlockSpec((heads, T, DA_PAIR), lambda b, h: (h, b, 0)),
            pl.BlockSpec((heads, T, DA_PAIR), lambda b, h: (groups + h, b, 0)),
            pl.BlockSpec((1, heads, DA_PAIR + ONES_ROWS, T), lambda b, h: (b, h, 0, 0)),
        ],
        out_specs=pl.BlockSpec((1, T, heads * DA_PAIR), lambda b, h: (b, 0, h)),
        out_shape=jax.ShapeDtypeStruct((B, T, D_MODEL), BF16),
        scratch_shapes=[
            pltpu.VMEM((2, DA_PAIR, 2 * tq), BF16),
            pltpu.VMEM((2, tk, 2 * tq), F32),
            pltpu.VMEM((2, 1, 2 * tq), F32),
            pltpu.VMEM((2, DA_PAIR + ONES_ROWS, 2 * tq), F32),
            pltpu.VMEM((1, 2 * tq), F32),
        ],
        compiler_params=_params("arbitrary", "arbitrary"),
        name="diff_attn",
    )(lam4, subln_g, qk, qk, vt)


def _merge_kernel(attn_ref, hf_ref, hb_ref, gy_ref, sa_ref, sl_ref, x_ref, pa_ref, plru_ref,
                  wo_ref, g_ref, b_ref, o_ref):
    a_proj = jnp.dot(attn_ref[...], pa_ref[...], preferred_element_type=F32)
    lru_out = ((hf_ref[...] + hb_ref[...]) * jax.nn.gelu(gy_ref[...].astype(F32))).astype(BF16)
    l_proj = jnp.dot(lru_out, plru_ref[...], preferred_element_type=F32)
    merged = (jax.nn.sigmoid(sa_ref[...].astype(F32)) * a_proj
              + jax.nn.sigmoid(sl_ref[...].astype(F32)) * l_proj)
    m = jnp.dot(merged.astype(BF16), wo_ref[...], preferred_element_type=F32)
    o_ref[...] = _layer_norm(DEEPNORM_ALPHA * x_ref[...] + m, g_ref[...], b_ref[...])


def _merge(attn2, hf2, hb2, act2, x2d, p_attn, p_lru, w_mix_out, ln_g, ln_b):
    M = x2d.shape[0]
    tm = ROW_TILE
    row = lambda seg: pl.BlockSpec((tm, D_MODEL), lambda i: (i, seg))
    return pl.pallas_call(
        _merge_kernel,
        grid=(M // tm,),
        in_specs=[
            row(0), row(0), row(0), row(0), row(1), row(2), row(0),
            _const_spec(p_attn.shape), _const_spec(p_lru.shape), _const_spec(w_mix_out.shape),
            _const_spec(ln_g.shape), _const_spec(ln_b.shape),
        ],
        out_specs=row(0),
        out_shape=jax.ShapeDtypeStruct((M, D_MODEL), F32),
        compiler_params=_params("arbitrary"),
        name="merge_ln1",
    )(attn2, hf2, hb2, act2, act2, act2, x2d, p_attn, p_lru, w_mix_out, ln_g, ln_b)


def _memkv_kernel(m_ref, w_ref, o_ref):
    o_ref[...] = jnp.dot(m_ref[...].astype(BF16), w_ref[...],
                         preferred_element_type=F32).astype(BF16)


def _memkv(mem2, xa_wkv):
    M = mem2.shape[0]
    tm = min(ROW_TILE, M)
    return pl.pallas_call(
        _memkv_kernel,
        grid=(M // tm, 2),
        in_specs=[
            pl.BlockSpec((tm, D_MODEL), lambda i, j: (i, 0)),
            pl.BlockSpec((D_MODEL, D_MODEL), lambda i, j: (0, j)),
        ],
        out_specs=pl.BlockSpec((tm, D_MODEL), lambda i, j: (i, j)),
        out_shape=jax.ShapeDtypeStruct((M, 2 * D_MODEL), BF16),
        compiler_params=_params("arbitrary", "arbitrary"),
        name="mem_kv",
    )(mem2, xa_wkv)


def _tail_kernel(x_ref, kv_ref, wq_ref, wo_ref, g2_ref, b2_ref, wi_ref, wf_ref, g3_ref, b3_ref,
                 o_ref):
    x1 = x_ref[...]
    q = (jnp.dot(x1.astype(BF16), wq_ref[...], preferred_element_type=F32)
         * (XA_HEAD_DIM ** -0.5)).astype(BF16)
    heads = []
    for h in range(XA_HEADS):
        sl = slice(h * XA_HEAD_DIM, (h + 1) * XA_HEAD_DIM)
        kh = kv_ref[0, :, sl]
        vh = kv_ref[0, :, D_MODEL + h * XA_HEAD_DIM:D_MODEL + (h + 1) * XA_HEAD_DIM]
        s = lax.dot_general(q[:, sl], kh, (((1,), (1,)), ((), ())), preferred_element_type=F32)
        e = jnp.exp(s - jnp.max(s, axis=-1, keepdims=True))
        p = e / jnp.sum(e, axis=-1, keepdims=True)
        heads.append(jnp.dot(p.astype(BF16), vh, preferred_element_type=F32).astype(BF16))
    xa = jnp.dot(jnp.concatenate(heads, axis=-1), wo_ref[...], preferred_element_type=F32)
    x2 = _layer_norm(DEEPNORM_ALPHA * x1 + xa, g2_ref[...], b2_ref[...])
    x2b = x2.astype(BF16)
    y = None
    for c0, cw in FFN_CHUNKS:
        g = jnp.dot(x2b, wi_ref[:, c0:c0 + cw], preferred_element_type=F32)
        u = jnp.dot(x2b, wi_ref[:, D_FF + c0:D_FF + c0 + cw], preferred_element_type=F32)
        hcb = (g * jax.nn.sigmoid(g) * u).astype(BF16)
        part = jnp.dot(hcb, wf_ref[c0:c0 + cw, :], preferred_element_type=F32)
        y = part if y is None else y + part
    o_ref[...] = _layer_norm(DEEPNORM_ALPHA * x2 + y, g3_ref[...], b3_ref[...])


def _tail(x1, kv3, T, xa_wq, xa_wo, ln2_g, ln2_b, ffn_w_in, ffn_w_out, ln3_g, ln3_b):
    M = x1.shape[0]
    tm = ROW_TILE
    tpb = T // tm
    row = pl.BlockSpec((tm, D_MODEL), lambda i: (i, 0))
    return pl.pallas_call(
        _tail_kernel,
        grid=(M // tm,),
        in_specs=[
            row,
            pl.BlockSpec((1, N_MEM, 2 * D_MODEL), lambda i: (i // tpb, 0, 0)),
            _const_spec(xa_wq.shape), _const_spec(xa_wo.shape),
            _const_spec(ln2_g.shape), _const_spec(ln2_b.shape),
            _const_spec(ffn_w_in.shape), _const_spec(ffn_w_out.shape),
            _const_spec(ln3_g.shape), _const_spec(ln3_b.shape),
        ],
        out_specs=row,
        out_shape=jax.ShapeDtypeStruct((M, D_MODEL), F32),
        compiler_params=_params("arbitrary"),
        name="xattn_ffn",
    )(x1, kv3, xa_wq, xa_wo, ln2_g, ln2_b, ffn_w_in, ffn_w_out, ln3_g, ln3_b)


def _rope_tables(T):
    inv = ROPE_THETA ** (-jnp.arange(0, ROT_DIM, 2, dtype=F32) / ROT_DIM)
    ang = jnp.arange(T, dtype=F32)[:, None] * inv[None, :]
    cos, sin = jnp.cos(ang), jnp.sin(ang)
    ones = jnp.ones((T, DA_HEAD_DIM - ROT_DIM), F32)
    zeros = jnp.zeros((T, DA_HEAD_DIM - ROT_DIM), F32)
    z8 = jnp.zeros((T, ROT_HALF), F32)
    cos64 = jnp.concatenate([cos, cos, ones], axis=1)
    sa64 = jnp.concatenate([z8, sin, zeros], axis=1)
    sb64 = jnp.concatenate([-sin, z8, zeros], axis=1)
    rep = LANES_V7X // DA_HEAD_DIM
    return (jnp.tile(cos64, (1, rep)), jnp.tile(sa64, (1, rep)), jnp.tile(sb64, (1, rep)))


def _trunk(x, mem, w):
    B, T, _ = x.shape
    M = B * T
    x2d = x.reshape(M, D_MODEL)
    cos_t, sa_t, sb_t = _rope_tables(T)
    qk, vt, xc, act = _proj(x2d, w["w_in"], cos_t, sa_t, sb_t, w["conv_w"], w["conv_b"], T)
    hf, hb = _lru(xc.reshape(B, T, D_MODEL), w["w_gates"], w["b_gates"], w["lru_a"])
    attn = _attention(qk, vt, w["lam4"], w["subln_g"])
    x1 = _merge(attn.reshape(M, D_MODEL), hf.reshape(M, D_MODEL), hb.reshape(M, D_MODEL), act,
                x2d, w["p_attn"], w["p_lru"], w["w_mix_out"], w["ln1_g"], w["ln1_b"])
    kv = _memkv(mem.reshape(B * N_MEM, D_MODEL), w["xa_wkv"])
    y = _tail(x1, kv.reshape(B, N_MEM, 2 * D_MODEL), T, w["xa_wq"], w["xa_wo"], w["ln2_g"],
              w["ln2_b"], w["ffn_w_in"], w["ffn_w_out"], w["ln3_g"], w["ln3_b"])
    return y.reshape(B, T, D_MODEL)


def _prepare_weights(w_in, lambda_q1, lambda_k1, lambda_q2, lambda_k2, subln_g, conv_w, conv_b,
                     lru_wa, lru_ba, lru_wx, lru_bx, lru_a, p_attn, p_lru, w_mix_out, ln1_g, ln1_b,
                     xa_wq, xa_wkv, xa_wo, ln2_g, ln2_b, ffn_w_in, ffn_w_out, ln3_g, ln3_b):
    row = lambda v: v[0].reshape(1, -1).astype(F32)
    return {
        "w_in": w_in[0].astype(BF16),
        "lam4": jnp.stack([lambda_q1[0], lambda_k1[0], lambda_q2[0], lambda_k2[0]]).astype(F32),
        "subln_g": row(subln_g),
        "conv_w": conv_w[0].astype(F32),
        "conv_b": row(conv_b),
        "w_gates": jnp.concatenate([lru_wa[0], lru_wx[0]], axis=-1).astype(BF16),
        "b_gates": jnp.concatenate([lru_ba[0], lru_bx[0]], axis=-1).astype(F32),
        "lru_a": lru_a[0].astype(F32),
        "p_attn": p_attn[0].astype(BF16),
        "p_lru": p_lru[0].astype(BF16),
        "w_mix_out": w_mix_out[0].astype(BF16),
        "ln1_g": row(ln1_g), "ln1_b": row(ln1_b),
        "xa_wq": xa_wq[0].astype(BF16),
        "xa_wkv": xa_wkv[0].astype(BF16),
        "xa_wo": xa_wo[0].astype(BF16),
        "ln2_g": row(ln2_g), "ln2_b": row(ln2_b),
        "ffn_w_in": ffn_w_in[0].astype(BF16),
        "ffn_w_out": ffn_w_out[0].astype(BF16),
        "ln3_g": row(ln3_g), "ln3_b": row(ln3_b),
    }


def kernel(x_prompt, x_sample, mem_prompt, mem_sample, w_in, lambda_q1, lambda_k1, lambda_q2, lambda_k2, subln_g, conv_w, conv_b, lru_wa, lru_ba, lru_wx, lru_bx, lru_a, p_attn, p_lru, w_mix_out, ln1_g, ln1_b, xa_wq, xa_wkv, xa_wo, ln2_g, ln2_b, ffn_w_in, ffn_w_out, ln3_g, ln3_b):
    w = _prepare_weights(w_in, lambda_q1, lambda_k1, lambda_q2, lambda_k2, subln_g, conv_w, conv_b,
                         lru_wa, lru_ba, lru_wx, lru_bx, lru_a, p_attn, p_lru, w_mix_out, ln1_g,
                         ln1_b, xa_wq, xa_wkv, xa_wo, ln2_g, ln2_b, ffn_w_in, ffn_w_out, ln3_g, ln3_b)
    return (_trunk(x_prompt, mem_prompt, w), _trunk(x_sample, mem_sample, w))
```

```python
import functools
import math

import jax
import jax.numpy as jnp
from jax import lax
from jax.experimental import pallas as pl
from jax.experimental.pallas import tpu as pltpu

F32 = jnp.float32
BF16 = jnp.bfloat16

D_MODEL = 1024
N_MEM = 256
DA_HEADS = 8
DA_HEAD_DIM = 64
DA_PAIR = 2 * DA_HEAD_DIM
ROT_DIM = DA_HEAD_DIM // 4
ROT_HALF = ROT_DIM // 2
ROPE_THETA = 500000.0
SUBLN_EPS = 1e-5
LRU_BLOCKS = 8
LRU_BLOCK_DIM = D_MODEL // LRU_BLOCKS
LRU_C = 8.0
XA_HEADS = 4
XA_HEAD_DIM = D_MODEL // XA_HEADS
D_FF = ((8 * D_MODEL + 3 * 256 - 1) // (3 * 256)) * 256
DEPTH = 1
DEEPNORM_ALPHA = (2.0 * DEPTH) ** 0.25
LN_EPS = 1e-5
LAMBDA_INIT = 0.8 - 0.6 * math.exp(-0.3 * 0)

LANES_V7X = 128
SUBLANES_V7X = 8
VMEM_LIMIT_V7X = 56 * 1024 * 1024

PROJ_ROWS = 1024
LRU_CHUNK = 512
ATTN_TQ = 512
ATTN_TK = 2048
ATTN_HEAD_TOKENS = 8192
ROW_TILE = 512
FFN_CHUNKS = ((0, 1024), (1024, 1024), (2048, D_FF - 2048))
NEG_BIG = -1e30
ONES_ROWS = 16


def _params(*sem):
    return pltpu.CompilerParams(dimension_semantics=sem, vmem_limit_bytes=VMEM_LIMIT_V7X)


def _const_spec(shape):
    nd = len(shape)
    return pl.BlockSpec(shape, lambda *_: (0,) * nd, pipeline_mode=pl.Buffered(1))


def _sigmoid(z):
    return 0.5 * jnp.tanh(0.5 * z) + 0.5


def _layer_norm(x, g, b):
    mu = jnp.mean(x, axis=-1, keepdims=True)
    xc = x - mu
    var = jnp.mean(xc * xc, axis=-1, keepdims=True)
    return xc * lax.rsqrt(var + LN_EPS) * g + b


def _proj_kernel(x_ref, xp_ref, xn_ref, w_ref, cos_ref, sa_ref, sb_ref, cw_ref, cb_ref,
                 qk_ref, vt_ref, xc_ref, *, tpb):
    i = pl.program_id(0)
    tm = x_ref.shape[0]
    s8 = SUBLANES_V7X
    ti = lax.rem(i, tpb)
    prev = jnp.where(ti > 0, xp_ref[...], 0.0)
    nxt = jnp.where(ti < tpb - 1, xn_ref[...], 0.0)
    xb_ext = jnp.concatenate([prev, x_ref[...], nxt], axis=0).astype(BF16)
    xb = xb_ext[s8:s8 + tm]

    def matmul(lhs, seg):
        return jnp.dot(lhs, w_ref[:, seg * D_MODEL:(seg + 1) * D_MODEL], preferred_element_type=F32)

    def rope_store(acc, scale, base):
        c, sa, sb = cos_ref[...], sa_ref[...], sb_ref[...]
        for h in range(DA_HEADS):
            blk = acc[:, h * DA_PAIR:(h + 1) * DA_PAIR]
            r = (blk * c + pltpu.roll(blk, ROT_HALF, 1) * sa
                 + pltpu.roll(blk, LANES_V7X - ROT_HALF, 1) * sb)
            qk_ref[base + h] = (r * scale).astype(BF16)

    xr = matmul(xb_ext, 3)
    n_ext = tm + 2 * s8
    conv = (cw_ref[0:1, :] * pltpu.roll(xr, 2, 0)
            + cw_ref[1:2, :] * pltpu.roll(xr, 1, 0)
            + cw_ref[2:3, :] * xr
            + cw_ref[3:4, :] * pltpu.roll(xr, n_ext - 1, 0))
    xc_ref[...] = conv[s8:s8 + tm] + cb_ref[...]
    rope_store(matmul(xb, 0), DA_HEAD_DIM ** -0.5 * math.log2(math.e), 0)
    rope_store(matmul(xb, 1), 1.0, DA_HEADS)
    v = matmul(xb, 2)
    ones = jnp.ones((ONES_ROWS, tm), BF16)
    for h in range(DA_HEADS):
        vt_ref[0, h, 0:DA_PAIR, :] = v[:, h * DA_PAIR:(h + 1) * DA_PAIR].T.astype(BF16)
        vt_ref[0, h, DA_PAIR:DA_PAIR + ONES_ROWS, :] = ones


def _proj(x2d, w_in, cos_t, sa_t, sb_t, conv_w, conv_b, T):
    M = x2d.shape[0]
    tm = min(PROJ_ROWS, T)
    tpb = T // tm
    t8 = tm // SUBLANES_V7X
    n8 = M // SUBLANES_V7X
    return pl.pallas_call(
        functools.partial(_proj_kernel, tpb=tpb),
        grid=(M // tm,),
        in_specs=[
            pl.BlockSpec((tm, D_MODEL), lambda i: (i, 0)),
            pl.BlockSpec((SUBLANES_V7X, D_MODEL), lambda i: (jnp.maximum(i * t8 - 1, 0), 0)),
            pl.BlockSpec((SUBLANES_V7X, D_MODEL), lambda i: (jnp.minimum((i + 1) * t8, n8 - 1), 0)),
            _const_spec(w_in.shape),
            pl.BlockSpec((tm, LANES_V7X), lambda i: (i % tpb, 0)),
            pl.BlockSpec((tm, LANES_V7X), lambda i: (i % tpb, 0)),
            pl.BlockSpec((tm, LANES_V7X), lambda i: (i % tpb, 0)),
            _const_spec(conv_w.shape), _const_spec(conv_b.shape),
        ],
        out_specs=[
            pl.BlockSpec((2 * DA_HEADS, tm, DA_PAIR), lambda i: (0, i, 0)),
            pl.BlockSpec((1, DA_HEADS, DA_PAIR + ONES_ROWS, tm),
                         lambda i: (i // tpb, 0, 0, i % tpb)),
            pl.BlockSpec((tm, D_MODEL), lambda i: (i, 0)),
        ],
        out_shape=[
            jax.ShapeDtypeStruct((2 * DA_HEADS, M, DA_PAIR), BF16),
            jax.ShapeDtypeStruct((M // T, DA_HEADS, DA_PAIR + ONES_ROWS, T), BF16),
            jax.ShapeDtypeStruct((M, D_MODEL), F32),
        ],
        compiler_params=_params("arbitrary"),
        name="proj",
    )(x2d, x2d, x2d, w_in, cos_t, sa_t, sb_t, conv_w, conv_b)


def _lru_kernel(xf_ref, xr_ref, w_ref, bias_ref, la_ref, hf_ref, hb_ref, a0_ref, u0_ref, a1_ref, u1_ref,
                carry_ref, *, tc, n_c):
    s = pl.program_id(0)

    @pl.when(s == 0)
    def _():
        a1_ref[...] = jnp.zeros_like(a1_ref)
        u1_ref[...] = jnp.zeros_like(u1_ref)
        carry_ref[...] = jnp.zeros_like(carry_ref)

    def prepare(x_ref, d, at_start, start_row, a_ref, u_ref):
        xc = x_ref[0]
        xcb = xc.astype(BF16)
        lam = la_ref[d:d + 1, :]
        decay = -LRU_C * (jnp.maximum(-lam, 0.0) + jnp.log1p(jnp.exp(-jnp.abs(lam))))
        for n in range(LRU_BLOCKS):
            sl = slice(n * LRU_BLOCK_DIM, (n + 1) * LRU_BLOCK_DIM)
            y = jnp.dot(xcb[:, sl], w_ref[d, n], preferred_element_type=F32)
            r = _sigmoid(y[:, :LRU_BLOCK_DIM] + bias_ref[d:d + 1, sl])
            gate = _sigmoid(
                y[:, LRU_BLOCK_DIM:]
                + bias_ref[d:d + 1, D_MODEL + n * LRU_BLOCK_DIM:D_MODEL + (n + 1) * LRU_BLOCK_DIM])
            a = jnp.exp(r * decay[:, sl])
            m2 = 1.0 - a * a
            mult = jnp.where(m2 > 0.0, m2 * lax.rsqrt(m2), 0.0)
            gx = gate * xc[:, sl]
            a_ref[d, :, sl] = a
            u_ref[d, :, sl] = mult * gx
            u_ref[d, start_row:start_row + 1, sl] = jnp.where(
                at_start, gx[start_row:start_row + 1], (mult * gx)[start_row:start_row + 1])

    def step(fill_a, fill_u, scan_a, scan_u):
        at_start = lax.rem(s, n_c) == 0
        prepare(xf_ref, 0, at_start, 0, fill_a, fill_u)
        prepare(xr_ref, 1, at_start, tc - 1, fill_a, fill_u)
        fresh = lax.rem(s - 1, n_c) == 0
        hf = jnp.where(fresh, 0.0, carry_ref[0:1, :])
        hb = jnp.where(fresh, 0.0, carry_ref[1:2, :])
        for t in range(tc):
            hf = scan_a[0, t:t + 1, :] * hf + scan_u[0, t:t + 1, :]
            hf_ref[0, t:t + 1, :] = hf
            tb = tc - 1 - t
            hb = scan_a[1, tb:tb + 1, :] * hb + scan_u[1, tb:tb + 1, :]
            hb_ref[0, tb:tb + 1, :] = hb
        carry_ref[0:1, :] = hf
        carry_ref[1:2, :] = hb

    @pl.when(lax.rem(s, 2) == 0)
    def _():
        step(a0_ref, u0_ref, a1_ref, u1_ref)

    @pl.when(lax.rem(s, 2) == 1)
    def _():
        step(a1_ref, u1_ref, a0_ref, u0_ref)


def _lru(xc3, w_gates, b_gates, lru_a):
    B, T, _ = xc3.shape
    tc = min(LRU_CHUNK, T)
    n_c = T // tc
    n_steps = B * n_c + 1
    last = B * n_c - 1

    def chunk(g, reverse):
        g = jnp.clip(g, 0, last)
        c = g % n_c
        return g // n_c, (n_c - 1 - c) if reverse else c, 0

    return pl.pallas_call(
        functools.partial(_lru_kernel, tc=tc, n_c=n_c),
        grid=(n_steps,),
        in_specs=[
            pl.BlockSpec((1, tc, D_MODEL), lambda s: chunk(s, False)),
            pl.BlockSpec((1, tc, D_MODEL), lambda s: chunk(s, True)),
            _const_spec(w_gates.shape), _const_spec(b_gates.shape), _const_spec(lru_a.shape),
        ],
        out_specs=[
            pl.BlockSpec((1, tc, D_MODEL), lambda s: chunk(s - 1, False)),
            pl.BlockSpec((1, tc, D_MODEL), lambda s: chunk(s - 1, True)),
        ],
        out_shape=[jax.ShapeDtypeStruct((B, T, D_MODEL), F32)] * 2,
        scratch_shapes=[
            pltpu.VMEM((2, tc, D_MODEL), F32),
            pltpu.VMEM((2, tc, D_MODEL), F32),
            pltpu.VMEM((2, tc, D_MODEL), F32),
            pltpu.VMEM((2, tc, D_MODEL), F32),
            pltpu.VMEM((2, D_MODEL), F32),
        ],
        compiler_params=_params("arbitrary"),
        name="lru",
    )(xc3, xc3, w_gates, b_gates, lru_a)


def _attn_kernel(lam_ref, g_ref, q_ref, k_ref, vt_ref, o_ref, qt_ref, st_ref, acc_ref, mc_ref,
                 m_ref, *, heads, n_q, n_kv, tq, tk):
    n_blk = heads * n_q

    def split(bi):
        if heads == 1:
            return 0, bi
        return lax.div(bi, n_q), lax.rem(bi, n_q)

    def prep_q(bi, slot):
        hh, qi = split(bi)
        off = pl.multiple_of(qi * tq, tq)
        qt = q_ref[hh, pl.ds(off, tq), :].astype(F32).T
        row = lax.broadcasted_iota(jnp.int32, qt.shape, 0)
        qt_ref[slot] = jnp.concatenate([jnp.where(row < DA_HEAD_DIM, qt, 0.0),
                                        jnp.where(row >= DA_HEAD_DIM, qt, 0.0)], axis=1).astype(BF16)

    def scores(qslot, hh, c, slot):
        off = pl.multiple_of(c * tk, tk)
        st = jnp.dot(k_ref[hh, pl.ds(off, tk), :], qt_ref[qslot], preferred_element_type=F32)
        st_ref[slot] = st
        mc_ref[slot] = jnp.max(st, axis=0, keepdims=True)

    def accumulate(aslot, hh, c, slot):
        off = pl.multiple_of(c * tk, tk)
        st = st_ref[slot]
        m_old = m_ref[...]
        m_new = jnp.maximum(m_old, mc_ref[slot])
        alpha = jnp.exp2(m_old - m_new)
        pt = jnp.exp2(st - m_new).astype(BF16)
        acc_ref[aslot] = alpha * acc_ref[aslot] + jnp.dot(vt_ref[0, hh, :, pl.ds(off, tk)], pt,
                                                         preferred_element_type=F32)
        m_ref[...] = m_new

    lam = (jnp.exp(jnp.sum(lam_ref[0:1, :] * lam_ref[1:2, :], axis=-1, keepdims=True))
           - jnp.exp(jnp.sum(lam_ref[2:3, :] * lam_ref[3:4, :], axis=-1, keepdims=True))
           + LAMBDA_INIT)

    def finalize(bi, aslot):
        hh, qi = split(bi)
        ot = (acc_ref[aslot, 0:DA_PAIR, 0:tq] / acc_ref[aslot, DA_PAIR:DA_PAIR + 1, 0:tq]
              - lam * (acc_ref[aslot, 0:DA_PAIR, tq:2 * tq]
                       / acc_ref[aslot, DA_PAIR:DA_PAIR + 1, tq:2 * tq]))
        o = ot.T
        o = o * lax.rsqrt(jnp.mean(o * o, axis=-1, keepdims=True) + SUBLN_EPS) * g_ref[...]
        off = pl.multiple_of(qi * tq, tq)
        col = hh * DA_PAIR if heads == 1 else pl.multiple_of(hh * DA_PAIR, DA_PAIR)
        o_ref[0, pl.ds(off, tq), pl.ds(col, DA_PAIR)] = (o * (1.0 - LAMBDA_INIT)).astype(BF16)

    prep_q(0, 0)
    scores(0, 0, 0, 0)
    acc_ref[1] = jnp.ones(acc_ref.shape[1:], F32)

    def q_block(bi, qs):
        hh, _ = split(bi)
        nxt = jnp.minimum(bi + 1, n_blk - 1)
        hh_nxt, _ = split(nxt)
        acc_ref[qs] = jnp.zeros(acc_ref.shape[1:], F32)
        m_ref[...] = jnp.full(m_ref.shape, NEG_BIG, F32)

        def pair(j, _):
            scores(qs, hh, 2 * j + 1, 1)
            accumulate(qs, hh, 2 * j, 0)
            scores(qs, hh, 2 * j + 2, 0)
            accumulate(qs, hh, 2 * j + 1, 1)
            return 0

        lax.fori_loop(0, n_kv // 2 - 1, pair, 0)
        scores(qs, hh, n_kv - 1, 1)
        accumulate(qs, hh, n_kv - 2, 0)
        finalize(jnp.maximum(bi - 1, 0), 1 - qs)
        prep_q(nxt, 1 - qs)
        scores(1 - qs, hh_nxt, 0, 0)
        accumulate(qs, hh, n_kv - 1, 1)

    per_trip = 2 if n_kv > 2 else 4

    def block_group(j, _):
        for r in range(per_trip):
            q_block(per_trip * j + r, r % 2)
        return 0

    lax.fori_loop(0, n_blk // per_trip, block_group, 0)
    finalize(n_blk - 1, (n_blk - 1) % 2)


def _attention(qk, vt, lam4, subln_g):
    B, _, _, T = vt.shape
    tq = min(ATTN_TQ, T)
    tk = min(ATTN_TK, T // 2)
    assert T % (2 * tk) == 0 and T % tq == 0, "key chunks are consumed in pairs"
    heads = max(1, min(DA_HEADS, ATTN_HEAD_TOKENS // T))
    groups = DA_HEADS // heads
    return pl.pallas_call(
        functools.partial(_attn_kernel, heads=heads, n_q=T // tq, n_kv=T // tk, tq=tq, tk=tk),
        grid=(B, groups),
        in_specs=[
            _const_spec(lam4.shape), _const_spec(subln_g.shape),
            pl.BlockSpec((heads, T, DA_PAIR), lambda b, h: (h, b, 0)),
            pl.BlockSpec((heads, T, DA_PAIR), lambda b, h: (groups + h, b, 0)),
            pl.BlockSpec((1, heads, DA_PAIR + ONES_ROWS, T), lambda b, h: (b, h, 0, 0)),
        ],
        out_specs=pl.BlockSpec((1, T, heads * DA_PAIR), lambda b, h: (b, 0, h)),
        out_shape=jax.ShapeDtypeStruct((B, T, D_MODEL), BF16),
        scratch_shapes=[
            pltpu.VMEM((2, DA_PAIR, 2 * tq), BF16),
            pltpu.VMEM((2, tk, 2 * tq), F32),
            pltpu.VMEM((2, DA_PAIR + ONES_ROWS, 2 * tq), F32),
            pltpu.VMEM((2, 1, 2 * tq), F32),
            pltpu.VMEM((1, 2 * tq), F32),
        ],
        compiler_params=_params("arbitrary", "arbitrary"),
        name="diff_attn",
    )(lam4, subln_g, qk, qk, vt)


def _merge_kernel(attn_ref, hf_ref, hb_ref, x_ref, wg_ref, pa_ref, plru_ref, wo_ref, g_ref, b_ref,
                  o_ref):
    x = x_ref[...]
    xb = x.astype(BF16)

    def gate(j):
        return jnp.dot(xb, wg_ref[:, j * D_MODEL:(j + 1) * D_MODEL], preferred_element_type=F32)

    a_proj = jnp.dot(attn_ref[...], pa_ref[...], preferred_element_type=F32)
    lru_out = ((hf_ref[...] + hb_ref[...]) * jax.nn.gelu(gate(0))).astype(BF16)
    l_proj = jnp.dot(lru_out, plru_ref[...], preferred_element_type=F32)
    merged = jax.nn.sigmoid(gate(1)) * a_proj + jax.nn.sigmoid(gate(2)) * l_proj
    m = jnp.dot(merged.astype(BF16), wo_ref[...], preferred_element_type=F32)
    o_ref[...] = _layer_norm(DEEPNORM_ALPHA * x + m, g_ref[...], b_ref[...])


def _merge(attn2, hf2, hb2, x2d, w_gate, p_attn, p_lru, w_mix_out, ln_g, ln_b):
    M = x2d.shape[0]
    tm = ROW_TILE
    row = pl.BlockSpec((tm, D_MODEL), lambda i: (i, 0))
    return pl.pallas_call(
        _merge_kernel,
        grid=(M // tm,),
        in_specs=[
            row, row, row, row,
            _const_spec(w_gate.shape), _const_spec(p_attn.shape), _const_spec(p_lru.shape),
            _const_spec(w_mix_out.shape), _const_spec(ln_g.shape), _const_spec(ln_b.shape),
        ],
        out_specs=row,
        out_shape=jax.ShapeDtypeStruct((M, D_MODEL), F32),
        compiler_params=_params("arbitrary"),
        name="merge_ln1",
    )(attn2, hf2, hb2, x2d, w_gate, p_attn, p_lru, w_mix_out, ln_g, ln_b)


def _memkv_kernel(m_ref, w_ref, o_ref):
    o_ref[...] = jnp.dot(m_ref[...].astype(BF16), w_ref[...],
                         preferred_element_type=F32).astype(BF16)


def _memkv(mem2, xa_wkv):
    M = mem2.shape[0]
    tm = min(ROW_TILE, M)
    return pl.pallas_call(
        _memkv_kernel,
        grid=(M // tm, 2),
        in_specs=[
            pl.BlockSpec((tm, D_MODEL), lambda i, j: (i, 0)),
            pl.BlockSpec((D_MODEL, D_MODEL), lambda i, j: (0, j)),
        ],
        out_specs=pl.BlockSpec((tm, D_MODEL), lambda i, j: (i, j)),
        out_shape=jax.ShapeDtypeStruct((M, 2 * D_MODEL), BF16),
        compiler_params=_params("arbitrary", "arbitrary"),
        name="mem_kv",
    )(mem2, xa_wkv)


def _tail_kernel(x_ref, kv_ref, wq_ref, wo_ref, g2_ref, b2_ref, wi_ref, wf_ref, g3_ref, b3_ref,
                 o_ref):
    x1 = x_ref[...]
    q = (jnp.dot(x1.astype(BF16), wq_ref[...], preferred_element_type=F32)
         * (XA_HEAD_DIM ** -0.5)).astype(BF16)
    heads = []
    for h in range(XA_HEADS):
        sl = slice(h * XA_HEAD_DIM, (h + 1) * XA_HEAD_DIM)
        kh = kv_ref[0, :, sl]
        vh = kv_ref[0, :, D_MODEL + h * XA_HEAD_DIM:D_MODEL + (h + 1) * XA_HEAD_DIM]
        s = lax.dot_general(q[:, sl], kh, (((1,), (1,)), ((), ())), preferred_element_type=F32)
        e = jnp.exp(s - jnp.max(s, axis=-1, keepdims=True))
        p = e / jnp.sum(e, axis=-1, keepdims=True)
        heads.append(jnp.dot(p.astype(BF16), vh, preferred_element_type=F32).astype(BF16))
    xa = jnp.dot(jnp.concatenate(heads, axis=-1), wo_ref[...], preferred_element_type=F32)
    x2 = _layer_norm(DEEPNORM_ALPHA * x1 + xa, g2_ref[...], b2_ref[...])
    x2b = x2.astype(BF16)
    y = None
    for c0, cw in FFN_CHUNKS:
        g = jnp.dot(x2b, wi_ref[:, c0:c0 + cw], preferred_element_type=F32)
        u = jnp.dot(x2b, wi_ref[:, D_FF + c0:D_FF + c0 + cw], preferred_element_type=F32)
        hcb = (g * jax.nn.sigmoid(g) * u).astype(BF16)
        part = jnp.dot(hcb, wf_ref[c0:c0 + cw, :], preferred_element_type=F32)
        y = part if y is None else y + part
    o_ref[...] = _layer_norm(DEEPNORM_ALPHA * x2 + y, g3_ref[...], b3_ref[...])


def _tail(x1, kv3, T, xa_wq, xa_wo, ln2_g, ln2_b, ffn_w_in, ffn_w_out, ln3_g, ln3_b):
    M = x1.shape[0]
    tm = ROW_TILE
    tpb = T // tm
    row = pl.BlockSpec((tm, D_MODEL), lambda i: (i, 0))
    return pl.pallas_call(
        _tail_kernel,
        grid=(M // tm,),
        in_specs=[
            row,
            pl.BlockSpec((1, N_MEM, 2 * D_MODEL), lambda i: (i // tpb, 0, 0)),
            _const_spec(xa_wq.shape), _const_spec(xa_wo.shape),
            _const_spec(ln2_g.shape), _const_spec(ln2_b.shape),
            _const_spec(ffn_w_in.shape), _const_spec(ffn_w_out.shape),
            _const_spec(ln3_g.shape), _const_spec(ln3_b.shape),
        ],
        out_specs=row,
        out_shape=jax.ShapeDtypeStruct((M, D_MODEL), F32),
        compiler_params=_params("arbitrary"),
        name="xattn_ffn",
    )(x1, kv3, xa_wq, xa_wo, ln2_g, ln2_b, ffn_w_in, ffn_w_out, ln3_g, ln3_b)


def _rope_tables(T):
    inv = ROPE_THETA ** (-jnp.arange(0, ROT_DIM, 2, dtype=F32) / ROT_DIM)
    ang = jnp.arange(T, dtype=F32)[:, None] * inv[None, :]
    cos, sin = jnp.cos(ang), jnp.sin(ang)
    ones = jnp.ones((T, DA_HEAD_DIM - ROT_DIM), F32)
    zeros = jnp.zeros((T, DA_HEAD_DIM - ROT_DIM), F32)
    z8 = jnp.zeros((T, ROT_HALF), F32)
    cos64 = jnp.concatenate([cos, cos, ones], axis=1)
    sa64 = jnp.concatenate([z8, sin, zeros], axis=1)
    sb64 = jnp.concatenate([-sin, z8, zeros], axis=1)
    rep = LANES_V7X // DA_HEAD_DIM
    return (jnp.tile(cos64, (1, rep)), jnp.tile(sa64, (1, rep)), jnp.tile(sb64, (1, rep)))


def _trunk(x, mem, w):
    B, T, _ = x.shape
    M = B * T
    x2d = x.reshape(M, D_MODEL)
    cos_t, sa_t, sb_t = _rope_tables(T)
    qk, vt, xc = _proj(x2d, w["w_qkvx"], cos_t, sa_t, sb_t, w["conv_w"], w["conv_b"], T)
    hf, hb = _lru(xc.reshape(B, T, D_MODEL), w["w_gates"], w["b_gates"], w["lru_a"])
    attn = _attention(qk, vt, w["lam4"], w["subln_g"])
    x1 = _merge(attn.reshape(M, D_MODEL), hf.reshape(M, D_MODEL), hb.reshape(M, D_MODEL), x2d,
                w["w_gate"], w["p_attn"], w["p_lru"], w["w_mix_out"], w["ln1_g"], w["ln1_b"])
    kv = _memkv(mem.reshape(B * N_MEM, D_MODEL), w["xa_wkv"])
    y = _tail(x1, kv.reshape(B, N_MEM, 2 * D_MODEL), T, w["xa_wq"], w["xa_wo"], w["ln2_g"],
              w["ln2_b"], w["ffn_w_in"], w["ffn_w_out"], w["ln3_g"], w["ln3_b"])
    return y.reshape(B, T, D_MODEL)


def _prepare_weights(w_in, lambda_q1, lambda_k1, lambda_q2, lambda_k2, subln_g, conv_w, conv_b,
                     lru_wa, lru_ba, lru_wx, lru_bx, lru_a, p_attn, p_lru, w_mix_out, ln1_g, ln1_b,
                     xa_wq, xa_wkv, xa_wo, ln2_g, ln2_b, ffn_w_in, ffn_w_out, ln3_g, ln3_b):
    row = lambda v: v[0].reshape(1, -1).astype(F32)
    return {
        "w_qkvx": w_in[0, :, :4 * D_MODEL].astype(BF16),
        "w_gate": w_in[0, :, 4 * D_MODEL:].astype(BF16),
        "lam4": jnp.stack([lambda_q1[0], lambda_k1[0], lambda_q2[0], lambda_k2[0]]).astype(F32),
        "subln_g": row(subln_g),
        "conv_w": conv_w[0].astype(F32),
        "conv_b": row(conv_b),
        "w_gates": jnp.concatenate([lru_wa[0], lru_wx[0]], axis=-1).astype(BF16),
        "b_gates": jnp.concatenate([lru_ba[0], lru_bx[0]], axis=-1).astype(F32),
        "lru_a": lru_a[0].astype(F32),
        "p_attn": p_attn[0].astype(BF16),
        "p_lru": p_lru[0].astype(BF16),
        "w_mix_out": w_mix_out[0].astype(BF16),
        "ln1_g": row(ln1_g), "ln1_b": row(ln1_b),
        "xa_wq": xa_wq[0].astype(BF16),
        "xa_wkv": xa_wkv[0].astype(BF16),
        "xa_wo": xa_wo[0].astype(BF16),
        "ln2_g": row(ln2_g), "ln2_b": row(ln2_b),
        "ffn_w_in": ffn_w_in[0].astype(BF16),
        "ffn_w_out": ffn_w_out[0].astype(BF16),
        "ln3_g": row(ln3_g), "ln3_b": row(ln3_b),
    }


def kernel(x_prompt, x_sample, mem_prompt, mem_sample, w_in, lambda_q1, lambda_k1, lambda_q2, lambda_k2, subln_g, conv_w, conv_b, lru_wa, lru_ba, lru_wx, lru_bx, lru_a, p_attn, p_lru, w_mix_out, ln1_g, ln1_b, xa_wq, xa_wkv, xa_wo, ln2_g, ln2_b, ffn_w_in, ffn_w_out, ln3_g, ln3_b):
    w = _prepare_weights(w_in, lambda_q1, lambda_k1, lambda_q2, lambda_k2, subln_g, conv_w, conv_b,
                         lru_wa, lru_ba, lru_wx, lru_bx, lru_a, p_attn, p_lru, w_mix_out, ln1_g,
                         ln1_b, xa_wq, xa_wkv, xa_wo, ln2_g, ln2_b, ffn_w_in, ffn_w_out, ln3_g, ln3_b)
    return (_trunk(x_prompt, mem_prompt, w), _trunk(x_sample, mem_sample, w))
```

```python
import functools
import math

import jax
import jax.numpy as jnp
from jax import lax
from jax.experimental import pallas as pl
from jax.experimental.pallas import tpu as pltpu

F32 = jnp.float32
BF16 = jnp.bfloat16

D_MODEL = 1024
N_MEM = 256
DA_HEADS = 8
DA_HEAD_DIM = 64
DA_PAIR = 2 * DA_HEAD_DIM
ROT_DIM = DA_HEAD_DIM // 4
ROT_HALF = ROT_DIM // 2
ROPE_THETA = 500000.0
SUBLN_EPS = 1e-5
LRU_BLOCKS = 8
LRU_BLOCK_DIM = D_MODEL // LRU_BLOCKS
LRU_C = 8.0
XA_HEADS = 4
XA_HEAD_DIM = D_MODEL // XA_HEADS
D_FF = ((8 * D_MODEL + 3 * 256 - 1) // (3 * 256)) * 256
DEPTH = 1
DEEPNORM_ALPHA = (2.0 * DEPTH) ** 0.25
LN_EPS = 1e-5
LAMBDA_INIT = 0.8 - 0.6 * math.exp(-0.3 * 0)

LANES_V7X = 128
SUBLANES_V7X = 8
VMEM_LIMIT_V7X = 56 * 1024 * 1024

PROJ_ROWS = 1024
LRU_CHUNK = 512
ATTN_TQ = 512
ATTN_TK = 1024
ATTN_HEAD_TOKENS = 8192
ROW_TILE = 512
FFN_CHUNKS = ((0, 1024), (1024, 1024), (2048, D_FF - 2048))
NEG_BIG = -1e30
ONES_ROWS = 16


def _params(*sem):
    return pltpu.CompilerParams(dimension_semantics=sem, vmem_limit_bytes=VMEM_LIMIT_V7X)


def _const_spec(shape):
    nd = len(shape)
    return pl.BlockSpec(shape, lambda *_: (0,) * nd, pipeline_mode=pl.Buffered(1))


def _sigmoid(z):
    return 0.5 * jnp.tanh(0.5 * z) + 0.5


def _layer_norm(x, g, b):
    mu = jnp.mean(x, axis=-1, keepdims=True)
    xc = x - mu
    var = jnp.mean(xc * xc, axis=-1, keepdims=True)
    return xc * lax.rsqrt(var + LN_EPS) * g + b


def _proj_kernel(x_ref, xp_ref, xn_ref, w_ref, cos_ref, sa_ref, sb_ref, cw_ref, cb_ref,
                 qk_ref, vt_ref, xc_ref, *, tpb):
    i = pl.program_id(0)
    tm = x_ref.shape[0]
    s8 = SUBLANES_V7X
    ti = lax.rem(i, tpb)
    prev = jnp.where(ti > 0, xp_ref[...], 0.0)
    nxt = jnp.where(ti < tpb - 1, xn_ref[...], 0.0)
    xb_ext = jnp.concatenate([prev, x_ref[...], nxt], axis=0).astype(BF16)
    xb = xb_ext[s8:s8 + tm]

    def matmul(lhs, seg):
        return jnp.dot(lhs, w_ref[:, seg * D_MODEL:(seg + 1) * D_MODEL], preferred_element_type=F32)

    def rope_store(acc, scale, base):
        c, sa, sb = cos_ref[...], sa_ref[...], sb_ref[...]
        for h in range(DA_HEADS):
            blk = acc[:, h * DA_PAIR:(h + 1) * DA_PAIR]
            r = (blk * c + pltpu.roll(blk, ROT_HALF, 1) * sa
                 + pltpu.roll(blk, LANES_V7X - ROT_HALF, 1) * sb)
            qk_ref[base + h] = (r * scale).astype(BF16)

    xr = matmul(xb_ext, 3)
    n_ext = tm + 2 * s8
    conv = (cw_ref[0:1, :] * pltpu.roll(xr, 2, 0)
            + cw_ref[1:2, :] * pltpu.roll(xr, 1, 0)
            + cw_ref[2:3, :] * xr
            + cw_ref[3:4, :] * pltpu.roll(xr, n_ext - 1, 0))
    xc_ref[...] = conv[s8:s8 + tm] + cb_ref[...]
    rope_store(matmul(xb, 0), DA_HEAD_DIM ** -0.5 * math.log2(math.e), 0)
    rope_store(matmul(xb, 1), 1.0, DA_HEADS)
    v = matmul(xb, 2)
    ones = jnp.ones((ONES_ROWS, tm), BF16)
    for h in range(DA_HEADS):
        vt_ref[0, h, 0:DA_PAIR, :] = v[:, h * DA_PAIR:(h + 1) * DA_PAIR].T.astype(BF16)
        vt_ref[0, h, DA_PAIR:DA_PAIR + ONES_ROWS, :] = ones


def _proj(x2d, w_in, cos_t, sa_t, sb_t, conv_w, conv_b, T):
    M = x2d.shape[0]
    tm = min(PROJ_ROWS, T)
    tpb = T // tm
    t8 = tm // SUBLANES_V7X
    n8 = M // SUBLANES_V7X
    return pl.pallas_call(
        functools.partial(_proj_kernel, tpb=tpb),
        grid=(M // tm,),
        in_specs=[
            pl.BlockSpec((tm, D_MODEL), lambda i: (i, 0)),
            pl.BlockSpec((SUBLANES_V7X, D_MODEL), lambda i: (jnp.maximum(i * t8 - 1, 0), 0)),
            pl.BlockSpec((SUBLANES_V7X, D_MODEL), lambda i: (jnp.minimum((i + 1) * t8, n8 - 1), 0)),
            _const_spec(w_in.shape),
            pl.BlockSpec((tm, LANES_V7X), lambda i: (i % tpb, 0)),
            pl.BlockSpec((tm, LANES_V7X), lambda i: (i % tpb, 0)),
            pl.BlockSpec((tm, LANES_V7X), lambda i: (i % tpb, 0)),
            _const_spec(conv_w.shape), _const_spec(conv_b.shape),
        ],
        out_specs=[
            pl.BlockSpec((2 * DA_HEADS, tm, DA_PAIR), lambda i: (0, i, 0)),
            pl.BlockSpec((1, DA_HEADS, DA_PAIR + ONES_ROWS, tm),
                         lambda i: (i // tpb, 0, 0, i % tpb)),
            pl.BlockSpec((tm, D_MODEL), lambda i: (i, 0)),
        ],
        out_shape=[
            jax.ShapeDtypeStruct((2 * DA_HEADS, M, DA_PAIR), BF16),
            jax.ShapeDtypeStruct((M // T, DA_HEADS, DA_PAIR + ONES_ROWS, T), BF16),
            jax.ShapeDtypeStruct((M, D_MODEL), F32),
        ],
        compiler_params=_params("arbitrary"),
        name="proj",
    )(x2d, x2d, x2d, w_in, cos_t, sa_t, sb_t, conv_w, conv_b)


def _lru_kernel(xf_ref, xr_ref, w_ref, bias_ref, la_ref, hf_ref, hb_ref, a0_ref, u0_ref, a1_ref, u1_ref,
                carry_ref, *, tc, n_c):
    s = pl.program_id(0)

    @pl.when(s == 0)
    def _():
        a1_ref[...] = jnp.zeros_like(a1_ref)
        u1_ref[...] = jnp.zeros_like(u1_ref)
        carry_ref[...] = jnp.zeros_like(carry_ref)

    def prepare(x_ref, d, at_start, start_row, a_ref, u_ref):
        xc = x_ref[0]
        xcb = xc.astype(BF16)
        lam = la_ref[d:d + 1, :]
        decay = -LRU_C * (jnp.maximum(-lam, 0.0) + jnp.log1p(jnp.exp(-jnp.abs(lam))))
        for n in range(LRU_BLOCKS):
            sl = slice(n * LRU_BLOCK_DIM, (n + 1) * LRU_BLOCK_DIM)
            y = jnp.dot(xcb[:, sl], w_ref[d, n], preferred_element_type=F32)
            r = _sigmoid(y[:, :LRU_BLOCK_DIM] + bias_ref[d:d + 1, sl])
            gate = _sigmoid(
                y[:, LRU_BLOCK_DIM:]
                + bias_ref[d:d + 1, D_MODEL + n * LRU_BLOCK_DIM:D_MODEL + (n + 1) * LRU_BLOCK_DIM])
            a = jnp.exp(r * decay[:, sl])
            m2 = 1.0 - a * a
            mult = jnp.where(m2 > 0.0, m2 * lax.rsqrt(m2), 0.0)
            gx = gate * xc[:, sl]
            a_ref[d, :, sl] = a
            u_ref[d, :, sl] = mult * gx
            u_ref[d, start_row:start_row + 1, sl] = jnp.where(
                at_start, gx[start_row:start_row + 1], (mult * gx)[start_row:start_row + 1])

    def step(fill_a, fill_u, scan_a, scan_u):
        at_start = lax.rem(s, n_c) == 0
        prepare(xf_ref, 0, at_start, 0, fill_a, fill_u)
        prepare(xr_ref, 1, at_start, tc - 1, fill_a, fill_u)
        fresh = lax.rem(s - 1, n_c) == 0
        hf = jnp.where(fresh, 0.0, carry_ref[0:1, :])
        hb = jnp.where(fresh, 0.0, carry_ref[1:2, :])
        for t in range(tc):
            hf = scan_a[0, t:t + 1, :] * hf + scan_u[0, t:t + 1, :]
            hf_ref[0, t:t + 1, :] = hf
            tb = tc - 1 - t
            hb = scan_a[1, tb:tb + 1, :] * hb + scan_u[1, tb:tb + 1, :]
            hb_ref[0, tb:tb + 1, :] = hb
        carry_ref[0:1, :] = hf
        carry_ref[1:2, :] = hb

    @pl.when(lax.rem(s, 2) == 0)
    def _():
        step(a0_ref, u0_ref, a1_ref, u1_ref)

    @pl.when(lax.rem(s, 2) == 1)
    def _():
        step(a1_ref, u1_ref, a0_ref, u0_ref)


def _lru(xc3, w_gates, b_gates, lru_a):
    B, T, _ = xc3.shape
    tc = min(LRU_CHUNK, T)
    n_c = T // tc
    n_steps = B * n_c + 1
    last = B * n_c - 1

    def chunk(g, reverse):
        g = jnp.clip(g, 0, last)
        c = g % n_c
        return g // n_c, (n_c - 1 - c) if reverse else c, 0

    return pl.pallas_call(
        functools.partial(_lru_kernel, tc=tc, n_c=n_c),
        grid=(n_steps,),
        in_specs=[
            pl.BlockSpec((1, tc, D_MODEL), lambda s: chunk(s, False)),
            pl.BlockSpec((1, tc, D_MODEL), lambda s: chunk(s, True)),
            _const_spec(w_gates.shape), _const_spec(b_gates.shape), _const_spec(lru_a.shape),
        ],
        out_specs=[
            pl.BlockSpec((1, tc, D_MODEL), lambda s: chunk(s - 1, False)),
            pl.BlockSpec((1, tc, D_MODEL), lambda s: chunk(s - 1, True)),
        ],
        out_shape=[jax.ShapeDtypeStruct((B, T, D_MODEL), F32)] * 2,
        scratch_shapes=[
            pltpu.VMEM((2, tc, D_MODEL), F32),
            pltpu.VMEM((2, tc, D_MODEL), F32),
            pltpu.VMEM((2, tc, D_MODEL), F32),
            pltpu.VMEM((2, tc, D_MODEL), F32),
            pltpu.VMEM((2, D_MODEL), F32),
        ],
        compiler_params=_params("arbitrary"),
        name="lru",
    )(xc3, xc3, w_gates, b_gates, lru_a)


def _attn_kernel(lam_ref, g_ref, q_ref, k_ref, vt_ref, o_ref, qt_ref, st_ref, acc_ref, mc_ref,
                 m_ref, *, heads, n_q, n_kv, tq, tk):
    n_blk = heads * n_q

    def split(bi):
        if heads == 1:
            return 0, bi
        return lax.div(bi, n_q), lax.rem(bi, n_q)

    def prep_q(bi, slot):
        hh, qi = split(bi)
        off = pl.multiple_of(qi * tq, tq)
        qt = q_ref[hh, pl.ds(off, tq), :].astype(F32).T
        row = lax.broadcasted_iota(jnp.int32, qt.shape, 0)
        qt_ref[slot] = jnp.concatenate([jnp.where(row < DA_HEAD_DIM, qt, 0.0),
                                        jnp.where(row >= DA_HEAD_DIM, qt, 0.0)], axis=1).astype(BF16)

    def scores(qslot, hh, c, slot):
        off = pl.multiple_of(c * tk, tk)
        st = jnp.dot(k_ref[hh, pl.ds(off, tk), :], qt_ref[qslot], preferred_element_type=F32)
        st_ref[slot] = st
        mc_ref[slot] = jnp.max(st, axis=0, keepdims=True)

    def accumulate(aslot, hh, c, slot):
        off = pl.multiple_of(c * tk, tk)
        st = st_ref[slot]
        m_old = m_ref[...]
        m_new = jnp.maximum(m_old, mc_ref[slot])
        alpha = jnp.exp2(m_old - m_new)
        pt = jnp.exp2(st - m_new).astype(BF16)
        acc_ref[aslot] = alpha * acc_ref[aslot] + jnp.dot(vt_ref[0, hh, :, pl.ds(off, tk)], pt,
                                                         preferred_element_type=F32)
        m_ref[...] = m_new

    lam = (jnp.exp(jnp.sum(lam_ref[0:1, :] * lam_ref[1:2, :], axis=-1, keepdims=True))
           - jnp.exp(jnp.sum(lam_ref[2:3, :] * lam_ref[3:4, :], axis=-1, keepdims=True))
           + LAMBDA_INIT)

    def finalize(bi, aslot):
        hh, qi = split(bi)
        ot = (acc_ref[aslot, 0:DA_PAIR, 0:tq] / acc_ref[aslot, DA_PAIR:DA_PAIR + 1, 0:tq]
              - lam * (acc_ref[aslot, 0:DA_PAIR, tq:2 * tq]
                       / acc_ref[aslot, DA_PAIR:DA_PAIR + 1, tq:2 * tq]))
        o = ot.T
        o = o * lax.rsqrt(jnp.mean(o * o, axis=-1, keepdims=True) + SUBLN_EPS) * g_ref[...]
        off = pl.multiple_of(qi * tq, tq)
        col = hh * DA_PAIR if heads == 1 else pl.multiple_of(hh * DA_PAIR, DA_PAIR)
        o_ref[0, pl.ds(off, tq), pl.ds(col, DA_PAIR)] = (o * (1.0 - LAMBDA_INIT)).astype(BF16)

    prep_q(0, 0)
    scores(0, 0, 0, 0)
    acc_ref[1] = jnp.ones(acc_ref.shape[1:], F32)

    def q_block(bi, qs):
        hh, _ = split(bi)
        nxt = jnp.minimum(bi + 1, n_blk - 1)
        hh_nxt, _ = split(nxt)
        acc_ref[qs] = jnp.zeros(acc_ref.shape[1:], F32)
        m_ref[...] = jnp.full(m_ref.shape, NEG_BIG, F32)

        def pair(j, _):
            scores(qs, hh, 2 * j + 1, 1)
            accumulate(qs, hh, 2 * j, 0)
            scores(qs, hh, 2 * j + 2, 0)
            accumulate(qs, hh, 2 * j + 1, 1)
            return 0

        lax.fori_loop(0, n_kv // 2 - 1, pair, 0)
        scores(qs, hh, n_kv - 1, 1)
        accumulate(qs, hh, n_kv - 2, 0)
        finalize(jnp.maximum(bi - 1, 0), 1 - qs)
        prep_q(nxt, 1 - qs)
        scores(1 - qs, hh_nxt, 0, 0)
        accumulate(qs, hh, n_kv - 1, 1)

    per_trip = 4

    def block_group(j, _):
        for r in range(per_trip):
            q_block(per_trip * j + r, r % 2)
        return 0

    lax.fori_loop(0, n_blk // per_trip, block_group, 0)
    finalize(n_blk - 1, (n_blk - 1) % 2)


def _attention(qk, vt, lam4, subln_g):
    B, _, _, T = vt.shape
    tq = min(ATTN_TQ, T)
    tk = min(ATTN_TK, T // 2)
    assert T % (2 * tk) == 0 and T % tq == 0, "key chunks are consumed in pairs"
    heads = max(1, min(DA_HEADS, ATTN_HEAD_TOKENS // T))
    groups = DA_HEADS // heads
    return pl.pallas_call(
        functools.partial(_attn_kernel, heads=heads, n_q=T // tq, n_kv=T // tk, tq=tq, tk=tk),
        grid=(B, groups),
        in_specs=[
            _const_spec(lam4.shape), _const_spec(subln_g.shape),
            pl.BlockSpec((heads, T, DA_PAIR), lambda b, h: (h, b, 0)),
            pl.BlockSpec((heads, T, DA_PAIR), lambda b, h: (groups + h, b, 0)),
            pl.BlockSpec((1, heads, DA_PAIR + ONES_ROWS, T), lambda b, h: (b, h, 0, 0)),
        ],
        out_specs=pl.BlockSpec((1, T, heads * DA_PAIR), lambda b, h: (b, 0, h)),
        out_shape=jax.ShapeDtypeStruct((B, T, D_MODEL), BF16),
        scratch_shapes=[
            pltpu.VMEM((2, DA_PAIR, 2 * tq), BF16),
            pltpu.VMEM((2, tk, 2 * tq), F32),
            pltpu.VMEM((2, DA_PAIR + ONES_ROWS, 2 * tq), F32),
            pltpu.VMEM((2, 1, 2 * tq), F32),
            pltpu.VMEM((1, 2 * tq), F32),
        ],
        compiler_params=_params("arbitrary", "arbitrary"),
        name="diff_attn",
    )(lam4, subln_g, qk, qk, vt)


def _merge_kernel(attn_ref, hf_ref, hb_ref, x_ref, wg_ref, pa_ref, plru_ref, wo_ref, g_ref, b_ref,
                  o_ref):
    x = x_ref[...]
    xb = x.astype(BF16)

    def gate(j):
        return jnp.dot(xb, wg_ref[:, j * D_MODEL:(j + 1) * D_MODEL], preferred_element_type=F32)

    a_proj = jnp.dot(attn_ref[...], pa_ref[...], preferred_element_type=F32)
    lru_out = ((hf_ref[...] + hb_ref[...]) * jax.nn.gelu(gate(0))).astype(BF16)
    l_proj = jnp.dot(lru_out, plru_ref[...], preferred_element_type=F32)
    merged = jax.nn.sigmoid(gate(1)) * a_proj + jax.nn.sigmoid(gate(2)) * l_proj
    m = jnp.dot(merged.astype(BF16), wo_ref[...], preferred_element_type=F32)
    o_ref[...] = _layer_norm(DEEPNORM_ALPHA * x + m, g_ref[...], b_ref[...])


def _merge(attn2, hf2, hb2, x2d, w_gate, p_attn, p_lru, w_mix_out, ln_g, ln_b):
    M = x2d.shape[0]
    tm = ROW_TILE
    row = pl.BlockSpec((tm, D_MODEL), lambda i: (i, 0))
    return pl.pallas_call(
        _merge_kernel,
        grid=(M // tm,),
        in_specs=[
            row, row, row, row,
            _const_spec(w_gate.shape), _const_spec(p_attn.shape), _const_spec(p_lru.shape),
            _const_spec(w_mix_out.shape), _const_spec(ln_g.shape), _const_spec(ln_b.shape),
        ],
        out_specs=row,
        out_shape=jax.ShapeDtypeStruct((M, D_MODEL), F32),
        compiler_params=_params("arbitrary"),
        name="merge_ln1",
    )(attn2, hf2, hb2, x2d, w_gate, p_attn, p_lru, w_mix_out, ln_g, ln_b)


def _memkv_kernel(m_ref, w_ref, o_ref):
    o_ref[...] = jnp.dot(m_ref[...].astype(BF16), w_ref[...],
                         preferred_element_type=F32).astype(BF16)


def _memkv(mem2, xa_wkv):
    M = mem2.shape[0]
    tm = min(ROW_TILE, M)
    return pl.pallas_call(
        _memkv_kernel,
        grid=(M // tm, 2),
        in_specs=[
            pl.BlockSpec((tm, D_MODEL), lambda i, j: (i, 0)),
            pl.BlockSpec((D_MODEL, D_MODEL), lambda i, j: (0, j)),
        ],
        out_specs=pl.BlockSpec((tm, D_MODEL), lambda i, j: (i, j)),
        out_shape=jax.ShapeDtypeStruct((M, 2 * D_MODEL), BF16),
        compiler_params=_params("arbitrary", "arbitrary"),
        name="mem_kv",
    )(mem2, xa_wkv)


def _tail_kernel(x_ref, kv_ref, wq_ref, wo_ref, g2_ref, b2_ref, wi_ref, wf_ref, g3_ref, b3_ref,
                 o_ref):
    x1 = x_ref[...]
    q = (jnp.dot(x1.astype(BF16), wq_ref[...], preferred_element_type=F32)
         * (XA_HEAD_DIM ** -0.5)).astype(BF16)
    heads = []
    for h in range(XA_HEADS):
        sl = slice(h * XA_HEAD_DIM, (h + 1) * XA_HEAD_DIM)
        kh = kv_ref[0, :, sl]
        vh = kv_ref[0, :, D_MODEL + h * XA_HEAD_DIM:D_MODEL + (h + 1) * XA_HEAD_DIM]
        s = lax.dot_general(q[:, sl], kh, (((1,), (1,)), ((), ())), preferred_element_type=F32)
        e = jnp.exp(s - jnp.max(s, axis=-1, keepdims=True))
        p = e / jnp.sum(e, axis=-1, keepdims=True)
        heads.append(jnp.dot(p.astype(BF16), vh, preferred_element_type=F32).astype(BF16))
    xa = jnp.dot(jnp.concatenate(heads, axis=-1), wo_ref[...], preferred_element_type=F32)
    x2 = _layer_norm(DEEPNORM_ALPHA * x1 + xa, g2_ref[...], b2_ref[...])
    x2b = x2.astype(BF16)
    y = None
    for c0, cw in FFN_CHUNKS:
        g = jnp.dot(x2b, wi_ref[:, c0:c0 + cw], preferred_element_type=F32)
        u = jnp.dot(x2b, wi_ref[:, D_FF + c0:D_FF + c0 + cw], preferred_element_type=F32)
        hcb = (g * jax.nn.sigmoid(g) * u).astype(BF16)
        part = jnp.dot(hcb, wf_ref[c0:c0 + cw, :], preferred_element_type=F32)
        y = part if y is None else y + part
    o_ref[...] = _layer_norm(DEEPNORM_ALPHA * x2 + y, g3_ref[...], b3_ref[...])


def _tail(x1, kv3, T, xa_wq, xa_wo, ln2_g, ln2_b, ffn_w_in, ffn_w_out, ln3_g, ln3_b):
    M = x1.shape[0]
    tm = ROW_TILE
    tpb = T // tm
    row = pl.BlockSpec((tm, D_MODEL), lambda i: (i, 0))
    return pl.pallas_call(
        _tail_kernel,
        grid=(M // tm,),
        in_specs=[
            row,
            pl.BlockSpec((1, N_MEM, 2 * D_MODEL), lambda i: (i // tpb, 0, 0)),
            _const_spec(xa_wq.shape), _const_spec(xa_wo.shape),
            _const_spec(ln2_g.shape), _const_spec(ln2_b.shape),
            _const_spec(ffn_w_in.shape), _const_spec(ffn_w_out.shape),
            _const_spec(ln3_g.shape), _const_spec(ln3_b.shape),
        ],
        out_specs=row,
        out_shape=jax.ShapeDtypeStruct((M, D_MODEL), F32),
        compiler_params=_params("arbitrary"),
        name="xattn_ffn",
    )(x1, kv3, xa_wq, xa_wo, ln2_g, ln2_b, ffn_w_in, ffn_w_out, ln3_g, ln3_b)


def _rope_tables(T):
    inv = ROPE_THETA ** (-jnp.arange(0, ROT_DIM, 2, dtype=F32) / ROT_DIM)
    ang = jnp.arange(T, dtype=F32)[:, None] * inv[None, :]
    cos, sin = jnp.cos(ang), jnp.sin(ang)
    ones = jnp.ones((T, DA_HEAD_DIM - ROT_DIM), F32)
    zeros = jnp.zeros((T, DA_HEAD_DIM - ROT_DIM), F32)
    z8 = jnp.zeros((T, ROT_HALF), F32)
    cos64 = jnp.concatenate([cos, cos, ones], axis=1)
    sa64 = jnp.concatenate([z8, sin, zeros], axis=1)
    sb64 = jnp.concatenate([-sin, z8, zeros], axis=1)
    rep = LANES_V7X // DA_HEAD_DIM
    return (jnp.tile(cos64, (1, rep)), jnp.tile(sa64, (1, rep)), jnp.tile(sb64, (1, rep)))


def _trunk(x, mem, w):
    B, T, _ = x.shape
    M = B * T
    x2d = x.reshape(M, D_MODEL)
    cos_t, sa_t, sb_t = _rope_tables(T)
    qk, vt, xc = _proj(x2d, w["w_qkvx"], cos_t, sa_t, sb_t, w["conv_w"], w["conv_b"], T)
    hf, hb = _lru(xc.reshape(B, T, D_MODEL), w["w_gates"], w["b_gates"], w["lru_a"])
    attn = _attention(qk, vt, w["lam4"], w["subln_g"])
    x1 = _merge(attn.reshape(M, D_MODEL), hf.reshape(M, D_MODEL), hb.reshape(M, D_MODEL), x2d,
                w["w_gate"], w["p_attn"], w["p_lru"], w["w_mix_out"], w["ln1_g"], w["ln1_b"])
    kv = _memkv(mem.reshape(B * N_MEM, D_MODEL), w["xa_wkv"])
    y = _tail(x1, kv.reshape(B, N_MEM, 2 * D_MODEL), T, w["xa_wq"], w["xa_wo"], w["ln2_g"],
              w["ln2_b"], w["ffn_w_in"], w["ffn_w_out"], w["ln3_g"], w["ln3_b"])
    return y.reshape(B, T, D_MODEL)


def _prepare_weights(w_in, lambda_q1, lambda_k1, lambda_q2, lambda_k2, subln_g, conv_w, conv_b,
                     lru_wa, lru_ba, lru_wx, lru_bx, lru_a, p_attn, p_lru, w_mix_out, ln1_g, ln1_b,
                     xa_wq, xa_wkv, xa_wo, ln2_g, ln2_b, ffn_w_in, ffn_w_out, ln3_g, ln3_b):
    row = lambda v: v[0].reshape(1, -1).astype(F32)
    return {
        "w_qkvx": w_in[0, :, :4 * D_MODEL].astype(BF16),
        "w_gate": w_in[0, :, 4 * D_MODEL:].astype(BF16),
        "lam4": jnp.stack([lambda_q1[0], lambda_k1[0], lambda_q2[0], lambda_k2[0]]).astype(F32),
        "subln_g": row(subln_g),
        "conv_w": conv_w[0].astype(F32),
        "conv_b": row(conv_b),
        "w_gates": jnp.concatenate([lru_wa[0], lru_wx[0]], axis=-1).astype(BF16),
        "b_gates": jnp.concatenate([lru_ba[0], lru_bx[0]], axis=-1).astype(F32),
        "lru_a": lru_a[0].astype(F32),
        "p_attn": p_attn[0].astype(BF16),
        "p_lru": p_lru[0].astype(BF16),
        "w_mix_out": w_mix_out[0].astype(BF16),
        "ln1_g": row(ln1_g), "ln1_b": row(ln1_b),
        "xa_wq": xa_wq[0].astype(BF16),
        "xa_wkv": xa_wkv[0].astype(BF16),
        "xa_wo": xa_wo[0].astype(BF16),
        "ln2_g": row(ln2_g), "ln2_b": row(ln2_b),
        "ffn_w_in": ffn_w_in[0].astype(BF16),
        "ffn_w_out": ffn_w_out[0].astype(BF16),
        "ln3_g": row(ln3_g), "ln3_b": row(ln3_b),
    }


def kernel(x_prompt, x_sample, mem_prompt, mem_sample, w_in, lambda_q1, lambda_k1, lambda_q2, lambda_k2, subln_g, conv_w, conv_b, lru_wa, lru_ba, lru_wx, lru_bx, lru_a, p_attn, p_lru, w_mix_out, ln1_g, ln1_b, xa_wq, xa_wkv, xa_wo, ln2_g, ln2_b, ffn_w_in, ffn_w_out, ln3_g, ln3_b):
    w = _prepare_weights(w_in, lambda_q1, lambda_k1, lambda_q2, lambda_k2, subln_g, conv_w, conv_b,
                         lru_wa, lru_ba, lru_wx, lru_bx, lru_a, p_attn, p_lru, w_mix_out, ln1_g,
                         ln1_b, xa_wq, xa_wkv, xa_wo, ln2_g, ln2_b, ffn_w_in, ffn_w_out, ln3_g, ln3_b)
    return (_trunk(x_prompt, mem_prompt, w), _trunk(x_sample, mem_sample, w))
```

```python
import functools
import math

import jax
import jax.numpy as jnp
from jax import lax
from jax.experimental import pallas as pl
from jax.experimental.pallas import tpu as pltpu

F32 = jnp.float32
BF16 = jnp.bfloat16

D_MODEL = 1024
N_MEM = 256
DA_HEADS = 8
DA_HEAD_DIM = 64
DA_PAIR = 2 * DA_HEAD_DIM
ROT_DIM = DA_HEAD_DIM // 4
ROT_HALF = ROT_DIM // 2
ROPE_THETA = 500000.0
SUBLN_EPS = 1e-5
LRU_BLOCKS = 8
LRU_BLOCK_DIM = D_MODEL // LRU_BLOCKS
LRU_C = 8.0
XA_HEADS = 4
XA_HEAD_DIM = D_MODEL // XA_HEADS
D_FF = ((8 * D_MODEL + 3 * 256 - 1) // (3 * 256)) * 256
DEPTH = 1
DEEPNORM_ALPHA = (2.0 * DEPTH) ** 0.25
LN_EPS = 1e-5
LAMBDA_INIT = 0.8 - 0.6 * math.exp(-0.3 * 0)

LANES_V7X = 128
SUBLANES_V7X = 8
VMEM_LIMIT_V7X = 56 * 1024 * 1024

PROJ_ROWS = 1024
LRU_CHUNK = 512
ATTN_TQ = 512
ATTN_TK = 1024
ATTN_HEAD_TOKENS = 8192
ROW_TILE = 512
FFN_CHUNKS = ((0, 1024), (1024, 1024), (2048, D_FF - 2048))
NEG_BIG = -1e30
ONES_ROWS = 16


def _params(*sem):
    return pltpu.CompilerParams(dimension_semantics=sem, vmem_limit_bytes=VMEM_LIMIT_V7X)


def _const_spec(shape):
    nd = len(shape)
    return pl.BlockSpec(shape, lambda *_: (0,) * nd, pipeline_mode=pl.Buffered(1))


def _sigmoid(z):
    return 0.5 * jnp.tanh(0.5 * z) + 0.5


def _layer_norm(x, g, b):
    mu = jnp.mean(x, axis=-1, keepdims=True)
    xc = x - mu
    var = jnp.mean(xc * xc, axis=-1, keepdims=True)
    return xc * lax.rsqrt(var + LN_EPS) * g + b


def _proj_kernel(x_ref, xp_ref, xn_ref, w_ref, cos_ref, sa_ref, sb_ref, cw_ref, cb_ref,
                 qk_ref, vt_ref, xc_ref, *, tpb):
    i = pl.program_id(0)
    tm = x_ref.shape[0]
    s8 = SUBLANES_V7X
    ti = lax.rem(i, tpb)
    prev = jnp.where(ti > 0, xp_ref[...], 0.0)
    nxt = jnp.where(ti < tpb - 1, xn_ref[...], 0.0)
    xb_ext = jnp.concatenate([prev, x_ref[...], nxt], axis=0).astype(BF16)
    xb = xb_ext[s8:s8 + tm]

    def matmul(lhs, seg):
        return jnp.dot(lhs, w_ref[:, seg * D_MODEL:(seg + 1) * D_MODEL], preferred_element_type=F32)

    def rope_store(acc, scale, base):
        c, sa, sb = cos_ref[...], sa_ref[...], sb_ref[...]
        for h in range(DA_HEADS):
            blk = acc[:, h * DA_PAIR:(h + 1) * DA_PAIR]
            r = (blk * c + pltpu.roll(blk, ROT_HALF, 1) * sa
                 + pltpu.roll(blk, LANES_V7X - ROT_HALF, 1) * sb)
            qk_ref[base + h] = (r * scale).astype(BF16)

    xr = matmul(xb_ext, 3)
    n_ext = tm + 2 * s8
    conv = (cw_ref[0:1, :] * pltpu.roll(xr, 2, 0)
            + cw_ref[1:2, :] * pltpu.roll(xr, 1, 0)
            + cw_ref[2:3, :] * xr
            + cw_ref[3:4, :] * pltpu.roll(xr, n_ext - 1, 0))
    xc_ref[...] = conv[s8:s8 + tm] + cb_ref[...]
    rope_store(matmul(xb, 0), DA_HEAD_DIM ** -0.5 * math.log2(math.e), 0)
    rope_store(matmul(xb, 1), 1.0, DA_HEADS)
    v = matmul(xb, 2)
    ones = jnp.ones((ONES_ROWS, tm), BF16)
    for h in range(DA_HEADS):
        vt_ref[0, h, 0:DA_PAIR, :] = v[:, h * DA_PAIR:(h + 1) * DA_PAIR].T.astype(BF16)
        vt_ref[0, h, DA_PAIR:DA_PAIR + ONES_ROWS, :] = ones


def _proj(x2d, w_in, cos_t, sa_t, sb_t, conv_w, conv_b, T):
    M = x2d.shape[0]
    tm = min(PROJ_ROWS, T)
    tpb = T // tm
    t8 = tm // SUBLANES_V7X
    n8 = M // SUBLANES_V7X
    return pl.pallas_call(
        functools.partial(_proj_kernel, tpb=tpb),
        grid=(M // tm,),
        in_specs=[
            pl.BlockSpec((tm, D_MODEL), lambda i: (i, 0)),
            pl.BlockSpec((SUBLANES_V7X, D_MODEL), lambda i: (jnp.maximum(i * t8 - 1, 0), 0)),
            pl.BlockSpec((SUBLANES_V7X, D_MODEL), lambda i: (jnp.minimum((i + 1) * t8, n8 - 1), 0)),
            _const_spec(w_in.shape),
            pl.BlockSpec((tm, LANES_V7X), lambda i: (i % tpb, 0)),
            pl.BlockSpec((tm, LANES_V7X), lambda i: (i % tpb, 0)),
            pl.BlockSpec((tm, LANES_V7X), lambda i: (i % tpb, 0)),
            _const_spec(conv_w.shape), _const_spec(conv_b.shape),
        ],
        out_specs=[
            pl.BlockSpec((2 * DA_HEADS, tm, DA_PAIR), lambda i: (0, i, 0)),
            pl.BlockSpec((1, DA_HEADS, DA_PAIR + ONES_ROWS, tm),
                         lambda i: (i // tpb, 0, 0, i % tpb)),
            pl.BlockSpec((tm, D_MODEL), lambda i: (i, 0)),
        ],
        out_shape=[
            jax.ShapeDtypeStruct((2 * DA_HEADS, M, DA_PAIR), BF16),
            jax.ShapeDtypeStruct((M // T, DA_HEADS, DA_PAIR + ONES_ROWS, T), BF16),
            jax.ShapeDtypeStruct((M, D_MODEL), F32),
        ],
        compiler_params=_params("arbitrary"),
        name="proj",
    )(x2d, x2d, x2d, w_in, cos_t, sa_t, sb_t, conv_w, conv_b)


def _lru_kernel(xf_ref, xr_ref, w_ref, bias_ref, la_ref, hf_ref, hb_ref, a0_ref, u0_ref, a1_ref, u1_ref,
                carry_ref, *, tc, n_c):
    s = pl.program_id(0)

    @pl.when(s == 0)
    def _():
        a1_ref[...] = jnp.zeros_like(a1_ref)
        u1_ref[...] = jnp.zeros_like(u1_ref)
        carry_ref[...] = jnp.zeros_like(carry_ref)

    def prepare(x_ref, d, at_start, start_row, a_ref, u_ref):
        xc = x_ref[0]
        xcb = xc.astype(BF16)
        lam = la_ref[d:d + 1, :]
        decay = -LRU_C * (jnp.maximum(-lam, 0.0) + jnp.log1p(jnp.exp(-jnp.abs(lam))))
        for n in range(LRU_BLOCKS):
            sl = slice(n * LRU_BLOCK_DIM, (n + 1) * LRU_BLOCK_DIM)
            y = jnp.dot(xcb[:, sl], w_ref[d, n], preferred_element_type=F32)
            r = _sigmoid(y[:, :LRU_BLOCK_DIM] + bias_ref[d:d + 1, sl])
            gate = _sigmoid(
                y[:, LRU_BLOCK_DIM:]
                + bias_ref[d:d + 1, D_MODEL + n * LRU_BLOCK_DIM:D_MODEL + (n + 1) * LRU_BLOCK_DIM])
            a = jnp.exp(r * decay[:, sl])
            m2 = 1.0 - a * a
            mult = jnp.where(m2 > 0.0, m2 * lax.rsqrt(m2), 0.0)
            gx = gate * xc[:, sl]
            a_ref[d, :, sl] = a
            u_ref[d, :, sl] = mult * gx
            u_ref[d, start_row:start_row + 1, sl] = jnp.where(
                at_start, gx[start_row:start_row + 1], (mult * gx)[start_row:start_row + 1])

    def step(fill_a, fill_u, scan_a, scan_u):
        at_start = lax.rem(s, n_c) == 0
        prepare(xf_ref, 0, at_start, 0, fill_a, fill_u)
        prepare(xr_ref, 1, at_start, tc - 1, fill_a, fill_u)
        fresh = lax.rem(s - 1, n_c) == 0
        hf = jnp.where(fresh, 0.0, carry_ref[0:1, :])
        hb = jnp.where(fresh, 0.0, carry_ref[1:2, :])
        for t in range(tc):
            hf = scan_a[0, t:t + 1, :] * hf + scan_u[0, t:t + 1, :]
            hf_ref[0, t:t + 1, :] = hf
            tb = tc - 1 - t
            hb = scan_a[1, tb:tb + 1, :] * hb + scan_u[1, tb:tb + 1, :]
            hb_ref[0, tb:tb + 1, :] = hb
        carry_ref[0:1, :] = hf
        carry_ref[1:2, :] = hb

    @pl.when(lax.rem(s, 2) == 0)
    def _():
        step(a0_ref, u0_ref, a1_ref, u1_ref)

    @pl.when(lax.rem(s, 2) == 1)
    def _():
        step(a1_ref, u1_ref, a0_ref, u0_ref)


def _lru(xc3, w_gates, b_gates, lru_a):
    B, T, _ = xc3.shape
    tc = min(LRU_CHUNK, T)
    n_c = T // tc
    n_steps = B * n_c + 1
    last = B * n_c - 1

    def chunk(g, reverse):
        g = jnp.clip(g, 0, last)
        c = g % n_c
        return g // n_c, (n_c - 1 - c) if reverse else c, 0

    return pl.pallas_call(
        functools.partial(_lru_kernel, tc=tc, n_c=n_c),
        grid=(n_steps,),
        in_specs=[
            pl.BlockSpec((1, tc, D_MODEL), lambda s: chunk(s, False)),
            pl.BlockSpec((1, tc, D_MODEL), lambda s: chunk(s, True)),
            _const_spec(w_gates.shape), _const_spec(b_gates.shape), _const_spec(lru_a.shape),
        ],
        out_specs=[
            pl.BlockSpec((1, tc, D_MODEL), lambda s: chunk(s - 1, False)),
            pl.BlockSpec((1, tc, D_MODEL), lambda s: chunk(s - 1, True)),
        ],
        out_shape=[jax.ShapeDtypeStruct((B, T, D_MODEL), F32)] * 2,
        scratch_shapes=[
            pltpu.VMEM((2, tc, D_MODEL), F32),
            pltpu.VMEM((2, tc, D_MODEL), F32),
            pltpu.VMEM((2, tc, D_MODEL), F32),
            pltpu.VMEM((2, tc, D_MODEL), F32),
            pltpu.VMEM((2, D_MODEL), F32),
        ],
        compiler_params=_params("arbitrary"),
        name="lru",
    )(xc3, xc3, w_gates, b_gates, lru_a)


def _attn_kernel(lam_ref, g_ref, q_ref, k_ref, vt_ref, o_ref, qt_ref, st_ref, acc_ref, mc_ref,
                 m_ref, *, heads, n_q, n_kv, tq, tk):
    n_blk = heads * n_q

    def split(bi):
        if heads == 1:
            return 0, bi
        return lax.div(bi, n_q), lax.rem(bi, n_q)

    def prep_q(bi, slot):
        hh, qi = split(bi)
        off = pl.multiple_of(qi * tq, tq)
        qt = q_ref[hh, pl.ds(off, tq), :].astype(F32).T
        row = lax.broadcasted_iota(jnp.int32, qt.shape, 0)
        qt_ref[slot] = jnp.concatenate([jnp.where(row < DA_HEAD_DIM, qt, 0.0),
                                        jnp.where(row >= DA_HEAD_DIM, qt, 0.0)], axis=1).astype(BF16)

    def scores(qslot, hh, c, slot):
        off = pl.multiple_of(c * tk, tk)
        st = jnp.dot(k_ref[hh, pl.ds(off, tk), :], qt_ref[qslot], preferred_element_type=F32)
        st_ref[slot] = st
        mc_ref[slot] = jnp.max(st, axis=0, keepdims=True)

    def accumulate(aslot, hh, c, slot):
        off = pl.multiple_of(c * tk, tk)
        st = st_ref[slot]
        m_old = m_ref[...]
        m_new = jnp.maximum(m_old, mc_ref[slot])
        alpha = jnp.exp2(m_old - m_new)
        pt = jnp.exp2(st - m_new).astype(BF16)
        acc_ref[aslot] = alpha * acc_ref[aslot] + jnp.dot(vt_ref[0, hh, :, pl.ds(off, tk)], pt,
                                                         preferred_element_type=F32)
        m_ref[...] = m_new

    lam = (jnp.exp(jnp.sum(lam_ref[0:1, :] * lam_ref[1:2, :], axis=-1, keepdims=True))
           - jnp.exp(jnp.sum(lam_ref[2:3, :] * lam_ref[3:4, :], axis=-1, keepdims=True))
           + LAMBDA_INIT)

    def finalize(bi, aslot):
        hh, qi = split(bi)
        ot = (acc_ref[aslot, 0:DA_PAIR, 0:tq] / acc_ref[aslot, DA_PAIR:DA_PAIR + 1, 0:tq]
              - lam * (acc_ref[aslot, 0:DA_PAIR, tq:2 * tq]
                       / acc_ref[aslot, DA_PAIR:DA_PAIR + 1, tq:2 * tq]))
        o = ot.T
        o = o * lax.rsqrt(jnp.mean(o * o, axis=-1, keepdims=True) + SUBLN_EPS) * g_ref[...]
        off = pl.multiple_of(qi * tq, tq)
        col = hh * DA_PAIR if heads == 1 else pl.multiple_of(hh * DA_PAIR, DA_PAIR)
        o_ref[0, pl.ds(off, tq), pl.ds(col, DA_PAIR)] = (o * (1.0 - LAMBDA_INIT)).astype(BF16)

    prep_q(0, 0)
    scores(0, 0, 0, 0)
    acc_ref[1] = jnp.ones(acc_ref.shape[1:], F32)

    def q_block(bi, qs):
        hh, _ = split(bi)
        nxt = jnp.minimum(bi + 1, n_blk - 1)
        hh_nxt, _ = split(nxt)
        acc_ref[qs] = jnp.zeros(acc_ref.shape[1:], F32)
        m_ref[...] = jnp.full(m_ref.shape, NEG_BIG, F32)

        def pair(j, _):
            scores(qs, hh, 2 * j + 1, 1)
            accumulate(qs, hh, 2 * j, 0)
            scores(qs, hh, 2 * j + 2, 0)
            accumulate(qs, hh, 2 * j + 1, 1)
            return 0

        lax.fori_loop(0, n_kv // 2 - 1, pair, 0, unroll=True)
        scores(qs, hh, n_kv - 1, 1)
        accumulate(qs, hh, n_kv - 2, 0)
        finalize(jnp.maximum(bi - 1, 0), 1 - qs)
        prep_q(nxt, 1 - qs)
        scores(1 - qs, hh_nxt, 0, 0)
        accumulate(qs, hh, n_kv - 1, 1)

    per_trip = 2 if n_kv > 2 else 4

    def block_group(j, _):
        for r in range(per_trip):
            q_block(per_trip * j + r, r % 2)
        return 0

    lax.fori_loop(0, n_blk // per_trip, block_group, 0)
    finalize(n_blk - 1, (n_blk - 1) % 2)


def _attention(qk, vt, lam4, subln_g):
    B, _, _, T = vt.shape
    tq = min(ATTN_TQ, T)
    tk = min(ATTN_TK, T // 2)
    assert T % (2 * tk) == 0 and T % tq == 0, "key chunks are consumed in pairs"
    heads = max(1, min(DA_HEADS, ATTN_HEAD_TOKENS // T))
    groups = DA_HEADS // heads
    return pl.pallas_call(
        functools.partial(_attn_kernel, heads=heads, n_q=T // tq, n_kv=T // tk, tq=tq, tk=tk),
        grid=(B, groups),
        in_specs=[
            _const_spec(lam4.shape), _const_spec(subln_g.shape),
            pl.BlockSpec((heads, T, DA_PAIR), lambda b, h: (h, b, 0)),
            pl.BlockSpec((heads, T, DA_PAIR), lambda b, h: (groups + h, b, 0)),
            pl.BlockSpec((1, heads, DA_PAIR + ONES_ROWS, T), lambda b, h: (b, h, 0, 0)),
        ],
        out_specs=pl.BlockSpec((1, T, heads * DA_PAIR), lambda b, h: (b, 0, h)),
        out_shape=jax.ShapeDtypeStruct((B, T, D_MODEL), BF16),
        scratch_shapes=[
            pltpu.VMEM((2, DA_PAIR, 2 * tq), BF16),
            pltpu.VMEM((2, tk, 2 * tq), F32),
            pltpu.VMEM((2, DA_PAIR + ONES_ROWS, 2 * tq), F32),
            pltpu.VMEM((2, 1, 2 * tq), F32),
            pltpu.VMEM((1, 2 * tq), F32),
        ],
        compiler_params=_params("arbitrary", "arbitrary"),
        name="diff_attn",
    )(lam4, subln_g, qk, qk, vt)


def _merge_kernel(attn_ref, hf_ref, hb_ref, x_ref, wg_ref, pa_ref, plru_ref, wo_ref, g_ref, b_ref,
                  o_ref):
    x = x_ref[...]
    xb = x.astype(BF16)

    def gate(j):
        return jnp.dot(xb, wg_ref[:, j * D_MODEL:(j + 1) * D_MODEL], preferred_element_type=F32)

    a_proj = jnp.dot(attn_ref[...], pa_ref[...], preferred_element_type=F32)
    lru_out = ((hf_ref[...] + hb_ref[...]) * jax.nn.gelu(gate(0))).astype(BF16)
    l_proj = jnp.dot(lru_out, plru_ref[...], preferred_element_type=F32)
    merged = jax.nn.sigmoid(gate(1)) * a_proj + jax.nn.sigmoid(gate(2)) * l_proj
    m = jnp.dot(merged.astype(BF16), wo_ref[...], preferred_element_type=F32)
    o_ref[...] = _layer_norm(DEEPNORM_ALPHA * x + m, g_ref[...], b_ref[...])


def _merge(attn2, hf2, hb2, x2d, w_gate, p_attn, p_lru, w_mix_out, ln_g, ln_b):
    M = x2d.shape[0]
    tm = ROW_TILE
    row = pl.BlockSpec((tm, D_MODEL), lambda i: (i, 0))
    return pl.pallas_call(
        _merge_kernel,
        grid=(M // tm,),
        in_specs=[
            row, row, row, row,
            _const_spec(w_gate.shape), _const_spec(p_attn.shape), _const_spec(p_lru.shape),
            _const_spec(w_mix_out.shape), _const_spec(ln_g.shape), _const_spec(ln_b.shape),
        ],
        out_specs=row,
        out_shape=jax.ShapeDtypeStruct((M, D_MODEL), F32),
        compiler_params=_params("arbitrary"),
        name="merge_ln1",
    )(attn2, hf2, hb2, x2d, w_gate, p_attn, p_lru, w_mix_out, ln_g, ln_b)


def _memkv_kernel(m_ref, w_ref, o_ref):
    o_ref[...] = jnp.dot(m_ref[...].astype(BF16), w_ref[...],
                         preferred_element_type=F32).astype(BF16)


def _memkv(mem2, xa_wkv):
    M = mem2.shape[0]
    tm = min(ROW_TILE, M)
    return pl.pallas_call(
        _memkv_kernel,
        grid=(M // tm, 2),
        in_specs=[
            pl.BlockSpec((tm, D_MODEL), lambda i, j: (i, 0)),
            pl.BlockSpec((D_MODEL, D_MODEL), lambda i, j: (0, j)),
        ],
        out_specs=pl.BlockSpec((tm, D_MODEL), lambda i, j: (i, j)),
        out_shape=jax.ShapeDtypeStruct((M, 2 * D_MODEL), BF16),
        compiler_params=_params("arbitrary", "arbitrary"),
        name="mem_kv",
    )(mem2, xa_wkv)


def _tail_kernel(x_ref, kv_ref, wq_ref, wo_ref, g2_ref, b2_ref, wi_ref, wf_ref, g3_ref, b3_ref,
                 o_ref):
    x1 = x_ref[...]
    q = (jnp.dot(x1.astype(BF16), wq_ref[...], preferred_element_type=F32)
         * (XA_HEAD_DIM ** -0.5)).astype(BF16)
    heads = []
    for h in range(XA_HEADS):
        sl = slice(h * XA_HEAD_DIM, (h + 1) * XA_HEAD_DIM)
        kh = kv_ref[0, :, sl]
        vh = kv_ref[0, :, D_MODEL + h * XA_HEAD_DIM:D_MODEL + (h + 1) * XA_HEAD_DIM]
        s = lax.dot_general(q[:, sl], kh, (((1,), (1,)), ((), ())), preferred_element_type=F32)
        e = jnp.exp(s - jnp.max(s, axis=-1, keepdims=True))
        p = e / jnp.sum(e, axis=-1, keepdims=True)
        heads.append(jnp.dot(p.astype(BF16), vh, preferred_element_type=F32).astype(BF16))
    xa = jnp.dot(jnp.concatenate(heads, axis=-1), wo_ref[...], preferred_element_type=F32)
    x2 = _layer_norm(DEEPNORM_ALPHA * x1 + xa, g2_ref[...], b2_ref[...])
    x2b = x2.astype(BF16)
    y = None
    for c0, cw in FFN_CHUNKS:
        g = jnp.dot(x2b, wi_ref[:, c0:c0 + cw], preferred_element_type=F32)
        u = jnp.dot(x2b, wi_ref[:, D_FF + c0:D_FF + c0 + cw], preferred_element_type=F32)
        hcb = (g * jax.nn.sigmoid(g) * u).astype(BF16)
        part = jnp.dot(hcb, wf_ref[c0:c0 + cw, :], preferred_element_type=F32)
        y = part if y is None else y + part
    o_ref[...] = _layer_norm(DEEPNORM_ALPHA * x2 + y, g3_ref[...], b3_ref[...])


def _tail(x1, kv3, T, xa_wq, xa_wo, ln2_g, ln2_b, ffn_w_in, ffn_w_out, ln3_g, ln3_b):
    M = x1.shape[0]
    tm = ROW_TILE
    tpb = T // tm
    row = pl.BlockSpec((tm, D_MODEL), lambda i: (i, 0))
    return pl.pallas_call(
        _tail_kernel,
        grid=(M // tm,),
        in_specs=[
            row,
            pl.BlockSpec((1, N_MEM, 2 * D_MODEL), lambda i: (i // tpb, 0, 0)),
            _const_spec(xa_wq.shape), _const_spec(xa_wo.shape),
            _const_spec(ln2_g.shape), _const_spec(ln2_b.shape),
            _const_spec(ffn_w_in.shape), _const_spec(ffn_w_out.shape),
            _const_spec(ln3_g.shape), _const_spec(ln3_b.shape),
        ],
        out_specs=row,
        out_shape=jax.ShapeDtypeStruct((M, D_MODEL), F32),
        compiler_params=_params("arbitrary"),
        name="xattn_ffn",
    )(x1, kv3, xa_wq, xa_wo, ln2_g, ln2_b, ffn_w_in, ffn_w_out, ln3_g, ln3_b)


def _rope_tables(T):
    inv = ROPE_THETA ** (-jnp.arange(0, ROT_DIM, 2, dtype=F32) / ROT_DIM)
    ang = jnp.arange(T, dtype=F32)[:, None] * inv[None, :]
    cos, sin = jnp.cos(ang), jnp.sin(ang)
    ones = jnp.ones((T, DA_HEAD_DIM - ROT_DIM), F32)
    zeros = jnp.zeros((T, DA_HEAD_DIM - ROT_DIM), F32)
    z8 = jnp.zeros((T, ROT_HALF), F32)
    cos64 = jnp.concatenate([cos, cos, ones], axis=1)
    sa64 = jnp.concatenate([z8, sin, zeros], axis=1)
    sb64 = jnp.concatenate([-sin, z8, zeros], axis=1)
    rep = LANES_V7X // DA_HEAD_DIM
    return (jnp.tile(cos64, (1, rep)), jnp.tile(sa64, (1, rep)), jnp.tile(sb64, (1, rep)))


def _trunk(x, mem, w):
    B, T, _ = x.shape
    M = B * T
    x2d = x.reshape(M, D_MODEL)
    cos_t, sa_t, sb_t = _rope_tables(T)
    qk, vt, xc = _proj(x2d, w["w_qkvx"], cos_t, sa_t, sb_t, w["conv_w"], w["conv_b"], T)
    hf, hb = _lru(xc.reshape(B, T, D_MODEL), w["w_gates"], w["b_gates"], w["lru_a"])
    attn = _attention(qk, vt, w["lam4"], w["subln_g"])
    x1 = _merge(attn.reshape(M, D_MODEL), hf.reshape(M, D_MODEL), hb.reshape(M, D_MODEL), x2d,
                w["w_gate"], w["p_attn"], w["p_lru"], w["w_mix_out"], w["ln1_g"], w["ln1_b"])
    kv = _memkv(mem.reshape(B * N_MEM, D_MODEL), w["xa_wkv"])
    y = _tail(x1, kv.reshape(B, N_MEM, 2 * D_MODEL), T, w["xa_wq"], w["xa_wo"], w["ln2_g"],
              w["ln2_b"], w["ffn_w_in"], w["ffn_w_out"], w["ln3_g"], w["ln3_b"])
    return y.reshape(B, T, D_MODEL)


def _prepare_weights(w_in, lambda_q1, lambda_k1, lambda_q2, lambda_k2, subln_g, conv_w, conv_b,
                     lru_wa, lru_ba, lru_wx, lru_bx, lru_a, p_attn, p_lru, w_mix_out, ln1_g, ln1_b,
                     xa_wq, xa_wkv, xa_wo, ln2_g, ln2_b, ffn_w_in, ffn_w_out, ln3_g, ln3_b):
    row = lambda v: v[0].reshape(1, -1).astype(F32)
    return {
        "w_qkvx": w_in[0, :, :4 * D_MODEL].astype(BF16),
        "w_gate": w_in[0, :, 4 * D_MODEL:].astype(BF16),
        "lam4": jnp.stack([lambda_q1[0], lambda_k1[0], lambda_q2[0], lambda_k2[0]]).astype(F32),
        "subln_g": row(subln_g),
        "conv_w": conv_w[0].astype(F32),
        "conv_b": row(conv_b),
        "w_gates": jnp.concatenate([lru_wa[0], lru_wx[0]], axis=-1).astype(BF16),
        "b_gates": jnp.concatenate([lru_ba[0], lru_bx[0]], axis=-1).astype(F32),
        "lru_a": lru_a[0].astype(F32),
        "p_attn": p_attn[0].astype(BF16),
        "p_lru": p_lru[0].astype(BF16),
        "w_mix_out": w_mix_out[0].astype(BF16),
        "ln1_g": row(ln1_g), "ln1_b": row(ln1_b),
        "xa_wq": xa_wq[0].astype(BF16),
        "xa_wkv": xa_wkv[0].astype(BF16),
        "xa_wo": xa_wo[0].astype(BF16),
        "ln2_g": row(ln2_g), "ln2_b": row(ln2_b),
        "ffn_w_in": ffn_w_in[0].astype(BF16),
        "ffn_w_out": ffn_w_out[0].astype(BF16),
        "ln3_g": row(ln3_g), "ln3_b": row(ln3_b),
    }


def kernel(x_prompt, x_sample, mem_prompt, mem_sample, w_in, lambda_q1, lambda_k1, lambda_q2, lambda_k2, subln_g, conv_w, conv_b, lru_wa, lru_ba, lru_wx, lru_bx, lru_a, p_attn, p_lru, w_mix_out, ln1_g, ln1_b, xa_wq, xa_wkv, xa_wo, ln2_g, ln2_b, ffn_w_in, ffn_w_out, ln3_g, ln3_b):
    w = _prepare_weights(w_in, lambda_q1, lambda_k1, lambda_q2, lambda_k2, subln_g, conv_w, conv_b,
                         lru_wa, lru_ba, lru_wx, lru_bx, lru_a, p_attn, p_lru, w_mix_out, ln1_g,
                         ln1_b, xa_wq, xa_wkv, xa_wo, ln2_g, ln2_b, ffn_w_in, ffn_w_out, ln3_g, ln3_b)
    return (_trunk(x_prompt, mem_prompt, w), _trunk(x_sample, mem_sample, w))
```

```python
import functools
import math

import jax
import jax.numpy as jnp
from jax import lax
from jax.experimental import pallas as pl
from jax.experimental.pallas import tpu as pltpu

F32 = jnp.float32
BF16 = jnp.bfloat16

D_MODEL = 1024
N_MEM = 256
DA_HEADS = 8
DA_HEAD_DIM = 64
DA_PAIR = 2 * DA_HEAD_DIM
ROT_DIM = DA_HEAD_DIM // 4
ROT_HALF = ROT_DIM // 2
ROPE_THETA = 500000.0
SUBLN_EPS = 1e-5
LRU_BLOCKS = 8
LRU_BLOCK_DIM = D_MODEL // LRU_BLOCKS
LRU_C = 8.0
XA_HEADS = 4
XA_HEAD_DIM = D_MODEL // XA_HEADS
D_FF = ((8 * D_MODEL + 3 * 256 - 1) // (3 * 256)) * 256
DEPTH = 1
DEEPNORM_ALPHA = (2.0 * DEPTH) ** 0.25
LN_EPS = 1e-5
LAMBDA_INIT = 0.8 - 0.6 * math.exp(-0.3 * 0)

LANES_V7X = 128
SUBLANES_V7X = 8
VMEM_LIMIT_V7X = 56 * 1024 * 1024

PROJ_ROWS = 1024
LRU_CHUNK = 512
ATTN_TQ = 512
ATTN_TK = 1024
ATTN_HEAD_TOKENS = 8192
ROW_TILE = 512
FFN_CHUNKS = ((0, 1024), (1024, 1024), (2048, D_FF - 2048))
NEG_BIG = -1e30
ONES_ROWS = 16


def _params(*sem):
    return pltpu.CompilerParams(dimension_semantics=sem, vmem_limit_bytes=VMEM_LIMIT_V7X)


def _const_spec(shape):
    nd = len(shape)
    return pl.BlockSpec(shape, lambda *_: (0,) * nd, pipeline_mode=pl.Buffered(1))


def _sigmoid(z):
    return 0.5 * jnp.tanh(0.5 * z) + 0.5


def _layer_norm(x, g, b):
    mu = jnp.mean(x, axis=-1, keepdims=True)
    xc = x - mu
    var = jnp.mean(xc * xc, axis=-1, keepdims=True)
    return xc * lax.rsqrt(var + LN_EPS) * g + b


def _proj_kernel(x_ref, xp_ref, xn_ref, w_ref, cos_ref, sa_ref, sb_ref, cw_ref, cb_ref,
                 qk_ref, vt_ref, xc_ref, *, tpb):
    i = pl.program_id(0)
    tm = x_ref.shape[0]
    s8 = SUBLANES_V7X
    ti = lax.rem(i, tpb)
    prev = jnp.where(ti > 0, xp_ref[...], 0.0)
    nxt = jnp.where(ti < tpb - 1, xn_ref[...], 0.0)
    xb_ext = jnp.concatenate([prev, x_ref[...], nxt], axis=0).astype(BF16)
    xb = xb_ext[s8:s8 + tm]

    def matmul(lhs, seg):
        return jnp.dot(lhs, w_ref[:, seg * D_MODEL:(seg + 1) * D_MODEL], preferred_element_type=F32)

    def rope_store(acc, scale, base):
        c, sa, sb = cos_ref[...], sa_ref[...], sb_ref[...]
        for h in range(DA_HEADS):
            blk = acc[:, h * DA_PAIR:(h + 1) * DA_PAIR]
            r = (blk * c + pltpu.roll(blk, ROT_HALF, 1) * sa
                 + pltpu.roll(blk, LANES_V7X - ROT_HALF, 1) * sb)
            qk_ref[base + h] = (r * scale).astype(BF16)

    xr = matmul(xb_ext, 3)
    n_ext = tm + 2 * s8
    conv = (cw_ref[0:1, :] * pltpu.roll(xr, 2, 0)
            + cw_ref[1:2, :] * pltpu.roll(xr, 1, 0)
            + cw_ref[2:3, :] * xr
            + cw_ref[3:4, :] * pltpu.roll(xr, n_ext - 1, 0))
    xc_ref[...] = conv[s8:s8 + tm] + cb_ref[...]
    rope_store(matmul(xb, 0), DA_HEAD_DIM ** -0.5 * math.log2(math.e), 0)
    rope_store(matmul(xb, 1), 1.0, DA_HEADS)
    v = matmul(xb, 2)
    ones = jnp.ones((ONES_ROWS, tm), BF16)
    for h in range(DA_HEADS):
        vt_ref[0, h, 0:DA_PAIR, :] = v[:, h * DA_PAIR:(h + 1) * DA_PAIR].T.astype(BF16)
        vt_ref[0, h, DA_PAIR:DA_PAIR + ONES_ROWS, :] = ones


def _proj(x2d, w_in, cos_t, sa_t, sb_t, conv_w, conv_b, T):
    M = x2d.shape[0]
    tm = min(PROJ_ROWS, T)
    tpb = T // tm
    t8 = tm // SUBLANES_V7X
    n8 = M // SUBLANES_V7X
    return pl.pallas_call(
        functools.partial(_proj_kernel, tpb=tpb),
        grid=(M // tm,),
        in_specs=[
            pl.BlockSpec((tm, D_MODEL), lambda i: (i, 0)),
            pl.BlockSpec((SUBLANES_V7X, D_MODEL), lambda i: (jnp.maximum(i * t8 - 1, 0), 0)),
            pl.BlockSpec((SUBLANES_V7X, D_MODEL), lambda i: (jnp.minimum((i + 1) * t8, n8 - 1), 0)),
            _const_spec(w_in.shape),
            pl.BlockSpec((tm, LANES_V7X), lambda i: (i % tpb, 0)),
            pl.BlockSpec((tm, LANES_V7X), lambda i: (i % tpb, 0)),
            pl.BlockSpec((tm, LANES_V7X), lambda i: (i % tpb, 0)),
            _const_spec(conv_w.shape), _const_spec(conv_b.shape),
        ],
        out_specs=[
            pl.BlockSpec((2 * DA_HEADS, tm, DA_PAIR), lambda i: (0, i, 0)),
            pl.BlockSpec((1, DA_HEADS, DA_PAIR + ONES_ROWS, tm),
                         lambda i: (i // tpb, 0, 0, i % tpb)),
            pl.BlockSpec((tm, D_MODEL), lambda i: (i, 0)),
        ],
        out_shape=[
            jax.ShapeDtypeStruct((2 * DA_HEADS, M, DA_PAIR), BF16),
            jax.ShapeDtypeStruct((M // T, DA_HEADS, DA_PAIR + ONES_ROWS, T), BF16),
            jax.ShapeDtypeStruct((M, D_MODEL), F32),
        ],
        compiler_params=_params("arbitrary"),
        name="proj",
    )(x2d, x2d, x2d, w_in, cos_t, sa_t, sb_t, conv_w, conv_b)


def _lru_kernel(xf_ref, xr_ref, w_ref, bias_ref, la_ref, hf_ref, hb_ref, a0_ref, u0_ref, a1_ref, u1_ref,
                carry_ref, *, tc, n_c):
    s = pl.program_id(0)

    @pl.when(s == 0)
    def _():
        a1_ref[...] = jnp.zeros_like(a1_ref)
        u1_ref[...] = jnp.zeros_like(u1_ref)
        carry_ref[...] = jnp.zeros_like(carry_ref)

    def prepare(x_ref, d, at_start, start_row, a_ref, u_ref):
        xc = x_ref[0]
        xcb = xc.astype(BF16)
        lam = la_ref[d:d + 1, :]
        decay = -LRU_C * (jnp.maximum(-lam, 0.0) + jnp.log1p(jnp.exp(-jnp.abs(lam))))
        for n in range(LRU_BLOCKS):
            sl = slice(n * LRU_BLOCK_DIM, (n + 1) * LRU_BLOCK_DIM)
            y = jnp.dot(xcb[:, sl], w_ref[d, n], preferred_element_type=F32)
            r = _sigmoid(y[:, :LRU_BLOCK_DIM] + bias_ref[d:d + 1, sl])
            gate = _sigmoid(
                y[:, LRU_BLOCK_DIM:]
                + bias_ref[d:d + 1, D_MODEL + n * LRU_BLOCK_DIM:D_MODEL + (n + 1) * LRU_BLOCK_DIM])
            a = jnp.exp(r * decay[:, sl])
            m2 = 1.0 - a * a
            mult = jnp.where(m2 > 0.0, m2 * lax.rsqrt(m2), 0.0)
            gx = gate * xc[:, sl]
            a_ref[d, :, sl] = a
            u_ref[d, :, sl] = mult * gx
            u_ref[d, start_row:start_row + 1, sl] = jnp.where(
                at_start, gx[start_row:start_row + 1], (mult * gx)[start_row:start_row + 1])

    def step(fill_a, fill_u, scan_a, scan_u):
        at_start = lax.rem(s, n_c) == 0
        prepare(xf_ref, 0, at_start, 0, fill_a, fill_u)
        prepare(xr_ref, 1, at_start, tc - 1, fill_a, fill_u)
        fresh = lax.rem(s - 1, n_c) == 0
        hf = jnp.where(fresh, 0.0, carry_ref[0:1, :])
        hb = jnp.where(fresh, 0.0, carry_ref[1:2, :])
        for t in range(tc):
            hf = scan_a[0, t:t + 1, :] * hf + scan_u[0, t:t + 1, :]
            hf_ref[0, t:t + 1, :] = hf
            tb = tc - 1 - t
            hb = scan_a[1, tb:tb + 1, :] * hb + scan_u[1, tb:tb + 1, :]
            hb_ref[0, tb:tb + 1, :] = hb
        carry_ref[0:1, :] = hf
        carry_ref[1:2, :] = hb

    @pl.when(lax.rem(s, 2) == 0)
    def _():
        step(a0_ref, u0_ref, a1_ref, u1_ref)

    @pl.when(lax.rem(s, 2) == 1)
    def _():
        step(a1_ref, u1_ref, a0_ref, u0_ref)


def _lru(xc3, w_gates, b_gates, lru_a):
    B, T, _ = xc3.shape
    tc = min(LRU_CHUNK, T)
    n_c = T // tc
    n_steps = B * n_c + 1
    last = B * n_c - 1

    def chunk(g, reverse):
        g = jnp.clip(g, 0, last)
        c = g % n_c
        return g // n_c, (n_c - 1 - c) if reverse else c, 0

    return pl.pallas_call(
        functools.partial(_lru_kernel, tc=tc, n_c=n_c),
        grid=(n_steps,),
        in_specs=[
            pl.BlockSpec((1, tc, D_MODEL), lambda s: chunk(s, False)),
            pl.BlockSpec((1, tc, D_MODEL), lambda s: chunk(s, True)),
            _const_spec(w_gates.shape), _const_spec(b_gates.shape), _const_spec(lru_a.shape),
        ],
        out_specs=[
            pl.BlockSpec((1, tc, D_MODEL), lambda s: chunk(s - 1, False)),
            pl.BlockSpec((1, tc, D_MODEL), lambda s: chunk(s - 1, True)),
        ],
        out_shape=[jax.ShapeDtypeStruct((B, T, D_MODEL), F32)] * 2,
        scratch_shapes=[
            pltpu.VMEM((2, tc, D_MODEL), F32),
            pltpu.VMEM((2, tc, D_MODEL), F32),
            pltpu.VMEM((2, tc, D_MODEL), F32),
            pltpu.VMEM((2, tc, D_MODEL), F32),
            pltpu.VMEM((2, D_MODEL), F32),
        ],
        compiler_params=_params("arbitrary"),
        name="lru",
    )(xc3, xc3, w_gates, b_gates, lru_a)


def _attn_kernel(lam_ref, g_ref, q_ref, k_ref, vt_ref, o_ref, qt_ref, st_ref, acc_ref, mc_ref,
                 m_ref, *, heads, n_q, n_kv, tq, tk):
    n_blk = heads * n_q

    def split(bi):
        if heads == 1:
            return 0, bi
        return lax.div(bi, n_q), lax.rem(bi, n_q)

    def prep_q(bi, slot):
        hh, qi = split(bi)
        off = pl.multiple_of(qi * tq, tq)
        qt = q_ref[hh, pl.ds(off, tq), :].astype(F32).T
        row = lax.broadcasted_iota(jnp.int32, qt.shape, 0)
        qt_ref[slot] = jnp.concatenate([jnp.where(row < DA_HEAD_DIM, qt, 0.0),
                                        jnp.where(row >= DA_HEAD_DIM, qt, 0.0)], axis=1).astype(BF16)

    def scores(qslot, hh, c, slot):
        off = pl.multiple_of(c * tk, tk)
        st = jnp.dot(k_ref[hh, pl.ds(off, tk), :], qt_ref[qslot], preferred_element_type=F32)
        st_ref[slot] = st
        mc_ref[slot] = jnp.max(st, axis=0, keepdims=True)

    def accumulate(aslot, hh, c, slot):
        off = pl.multiple_of(c * tk, tk)
        st = st_ref[slot]
        m_old = m_ref[...]
        m_new = jnp.maximum(m_old, mc_ref[slot])
        alpha = jnp.exp2(m_old - m_new)
        pt = jnp.exp2(st - m_new).astype(BF16)
        acc_ref[aslot] = alpha * acc_ref[aslot] + jnp.dot(vt_ref[0, hh, :, pl.ds(off, tk)], pt,
                                                         preferred_element_type=F32)
        m_ref[...] = m_new

    lam = (jnp.exp(jnp.sum(lam_ref[0:1, :] * lam_ref[1:2, :], axis=-1, keepdims=True))
           - jnp.exp(jnp.sum(lam_ref[2:3, :] * lam_ref[3:4, :], axis=-1, keepdims=True))
           + LAMBDA_INIT)

    def finalize(bi, aslot):
        hh, qi = split(bi)
        ot = (acc_ref[aslot, 0:DA_PAIR, 0:tq] / acc_ref[aslot, DA_PAIR:DA_PAIR + 1, 0:tq]
              - lam * (acc_ref[aslot, 0:DA_PAIR, tq:2 * tq]
                       / acc_ref[aslot, DA_PAIR:DA_PAIR + 1, tq:2 * tq]))
        o = ot.T
        o = o * lax.rsqrt(jnp.mean(o * o, axis=-1, keepdims=True) + SUBLN_EPS) * g_ref[...]
        off = pl.multiple_of(qi * tq, tq)
        col = hh * DA_PAIR if heads == 1 else pl.multiple_of(hh * DA_PAIR, DA_PAIR)
        o_ref[0, pl.ds(off, tq), pl.ds(col, DA_PAIR)] = (o * (1.0 - LAMBDA_INIT)).astype(BF16)

    prep_q(0, 0)
    scores(0, 0, 0, 0)
    acc_ref[1] = jnp.ones(acc_ref.shape[1:], F32)

    def q_block(bi, qs):
        hh, _ = split(bi)
        nxt = jnp.minimum(bi + 1, n_blk - 1)
        hh_nxt, _ = split(nxt)
        acc_ref[qs] = jnp.zeros(acc_ref.shape[1:], F32)
        m_ref[...] = jnp.full(m_ref.shape, NEG_BIG, F32)

        def pair(j, _):
            scores(qs, hh, 2 * j + 1, 1)
            accumulate(qs, hh, 2 * j, 0)
            scores(qs, hh, 2 * j + 2, 0)
            accumulate(qs, hh, 2 * j + 1, 1)
            return 0

        lax.fori_loop(0, n_kv // 2 - 1, pair, 0, unroll=True)
        scores(qs, hh, n_kv - 1, 1)
        accumulate(qs, hh, n_kv - 2, 0)
        finalize(jnp.maximum(bi - 1, 0), 1 - qs)
        prep_q(nxt, 1 - qs)
        scores(1 - qs, hh_nxt, 0, 0)
        accumulate(qs, hh, n_kv - 1, 1)

    per_trip = 2 if n_kv > 2 else 8

    def block_group(j, _):
        for r in range(per_trip):
            q_block(per_trip * j + r, r % 2)
        return 0

    lax.fori_loop(0, n_blk // per_trip, block_group, 0)
    finalize(n_blk - 1, (n_blk - 1) % 2)


def _attention(qk, vt, lam4, subln_g):
    B, _, _, T = vt.shape
    tq = min(ATTN_TQ, T)
    tk = min(ATTN_TK, T // 2)
    assert T % (2 * tk) == 0 and T % tq == 0, "key chunks are consumed in pairs"
    heads = max(1, min(DA_HEADS, ATTN_HEAD_TOKENS // T))
    groups = DA_HEADS // heads
    return pl.pallas_call(
        functools.partial(_attn_kernel, heads=heads, n_q=T // tq, n_kv=T // tk, tq=tq, tk=tk),
        grid=(B, groups),
        in_specs=[
            _const_spec(lam4.shape), _const_spec(subln_g.shape),
            pl.BlockSpec((heads, T, DA_PAIR), lambda b, h: (h, b, 0)),
            pl.BlockSpec((heads, T, DA_PAIR), lambda b, h: (groups + h, b, 0)),
            pl.BlockSpec((1, heads, DA_PAIR + ONES_ROWS, T), lambda b, h: (b, h, 0, 0)),
        ],
        out_specs=pl.BlockSpec((1, T, heads * DA_PAIR), lambda b, h: (b, 0, h)),
        out_shape=jax.ShapeDtypeStruct((B, T, D_MODEL), BF16),
        scratch_shapes=[
            pltpu.VMEM((2, DA_PAIR, 2 * tq), BF16),
            pltpu.VMEM((2, tk, 2 * tq), F32),
            pltpu.VMEM((2, DA_PAIR + ONES_ROWS, 2 * tq), F32),
            pltpu.VMEM((2, 1, 2 * tq), F32),
            pltpu.VMEM((1, 2 * tq), F32),
        ],
        compiler_params=_params("arbitrary", "arbitrary"),
        name="diff_attn",
    )(lam4, subln_g, qk, qk, vt)


def _merge_kernel(attn_ref, hf_ref, hb_ref, x_ref, wg_ref, pa_ref, plru_ref, wo_ref, g_ref, b_ref,
                  o_ref):
    x = x_ref[...]
    xb = x.astype(BF16)

    def gate(j):
        return jnp.dot(xb, wg_ref[:, j * D_MODEL:(j + 1) * D_MODEL], preferred_element_type=F32)

    a_proj = jnp.dot(attn_ref[...], pa_ref[...], preferred_element_type=F32)
    lru_out = ((hf_ref[...] + hb_ref[...]) * jax.nn.gelu(gate(0))).astype(BF16)
    l_proj = jnp.dot(lru_out, plru_ref[...], preferred_element_type=F32)
    merged = jax.nn.sigmoid(gate(1)) * a_proj + jax.nn.sigmoid(gate(2)) * l_proj
    m = jnp.dot(merged.astype(BF16), wo_ref[...], preferred_element_type=F32)
    o_ref[...] = _layer_norm(DEEPNORM_ALPHA * x + m, g_ref[...], b_ref[...])


def _merge(attn2, hf2, hb2, x2d, w_gate, p_attn, p_lru, w_mix_out, ln_g, ln_b):
    M = x2d.shape[0]
    tm = ROW_TILE
    row = pl.BlockSpec((tm, D_MODEL), lambda i: (i, 0))
    return pl.pallas_call(
        _merge_kernel,
        grid=(M // tm,),
        in_specs=[
            row, row, row, row,
            _const_spec(w_gate.shape), _const_spec(p_attn.shape), _const_spec(p_lru.shape),
            _const_spec(w_mix_out.shape), _const_spec(ln_g.shape), _const_spec(ln_b.shape),
        ],
        out_specs=row,
        out_shape=jax.ShapeDtypeStruct((M, D_MODEL), F32),
        compiler_params=_params("arbitrary"),
        name="merge_ln1",
    )(attn2, hf2, hb2, x2d, w_gate, p_attn, p_lru, w_mix_out, ln_g, ln_b)


def _memkv_kernel(m_ref, w_ref, o_ref):
    o_ref[...] = jnp.dot(m_ref[...].astype(BF16), w_ref[...],
                         preferred_element_type=F32).astype(BF16)


def _memkv(mem2, xa_wkv):
    M = mem2.shape[0]
    tm = min(ROW_TILE, M)
    return pl.pallas_call(
        _memkv_kernel,
        grid=(M // tm, 2),
        in_specs=[
            pl.BlockSpec((tm, D_MODEL), lambda i, j: (i, 0)),
            pl.BlockSpec((D_MODEL, D_MODEL), lambda i, j: (0, j)),
        ],
        out_specs=pl.BlockSpec((tm, D_MODEL), lambda i, j: (i, j)),
        out_shape=jax.ShapeDtypeStruct((M, 2 * D_MODEL), BF16),
        compiler_params=_params("arbitrary", "arbitrary"),
        name="mem_kv",
    )(mem2, xa_wkv)


def _tail_kernel(x_ref, kv_ref, wq_ref, wo_ref, g2_ref, b2_ref, wi_ref, wf_ref, g3_ref, b3_ref,
                 o_ref):
    x1 = x_ref[...]
    q = (jnp.dot(x1.astype(BF16), wq_ref[...], preferred_element_type=F32)
         * (XA_HEAD_DIM ** -0.5)).astype(BF16)
    heads = []
    for h in range(XA_HEADS):
        sl = slice(h * XA_HEAD_DIM, (h + 1) * XA_HEAD_DIM)
        kh = kv_ref[0, :, sl]
        vh = kv_ref[0, :, D_MODEL + h * XA_HEAD_DIM:D_MODEL + (h + 1) * XA_HEAD_DIM]
        s = lax.dot_general(q[:, sl], kh, (((1,), (1,)), ((), ())), preferred_element_type=F32)
        e = jnp.exp(s - jnp.max(s, axis=-1, keepdims=True))
        p = e / jnp.sum(e, axis=-1, keepdims=True)
        heads.append(jnp.dot(p.astype(BF16), vh, preferred_element_type=F32).astype(BF16))
    xa = jnp.dot(jnp.concatenate(heads, axis=-1), wo_ref[...], preferred_element_type=F32)
    x2 = _layer_norm(DEEPNORM_ALPHA * x1 + xa, g2_ref[...], b2_ref[...])
    x2b = x2.astype(BF16)
    y = None
    for c0, cw in FFN_CHUNKS:
        g = jnp.dot(x2b, wi_ref[:, c0:c0 + cw], preferred_element_type=F32)
        u = jnp.dot(x2b, wi_ref[:, D_FF + c0:D_FF + c0 + cw], preferred_element_type=F32)
        hcb = (g * jax.nn.sigmoid(g) * u).astype(BF16)
        part = jnp.dot(hcb, wf_ref[c0:c0 + cw, :], preferred_element_type=F32)
        y = part if y is None else y + part
    o_ref[...] = _layer_norm(DEEPNORM_ALPHA * x2 + y, g3_ref[...], b3_ref[...])


def _tail(x1, kv3, T, xa_wq, xa_wo, ln2_g, ln2_b, ffn_w_in, ffn_w_out, ln3_g, ln3_b):
    M = x1.shape[0]
    tm = ROW_TILE
    tpb = T // tm
    row = pl.BlockSpec((tm, D_MODEL), lambda i: (i, 0))
    return pl.pallas_call(
        _tail_kernel,
        grid=(M // tm,),
        in_specs=[
            row,
            pl.BlockSpec((1, N_MEM, 2 * D_MODEL), lambda i: (i // tpb, 0, 0)),
            _const_spec(xa_wq.shape), _const_spec(xa_wo.shape),
            _const_spec(ln2_g.shape), _const_spec(ln2_b.shape),
            _const_spec(ffn_w_in.shape), _const_spec(ffn_w_out.shape),
            _const_spec(ln3_g.shape), _const_spec(ln3_b.shape),
        ],
        out_specs=row,
        out_shape=jax.ShapeDtypeStruct((M, D_MODEL), F32),
        compiler_params=_params("arbitrary"),
        name="xattn_ffn",
    )(x1, kv3, xa_wq, xa_wo, ln2_g, ln2_b, ffn_w_in, ffn_w_out, ln3_g, ln3_b)


def _rope_tables(T):
    inv = ROPE_THETA ** (-jnp.arange(0, ROT_DIM, 2, dtype=F32) / ROT_DIM)
    ang = jnp.arange(T, dtype=F32)[:, None] * inv[None, :]
    cos, sin = jnp.cos(ang), jnp.sin(ang)
    ones = jnp.ones((T, DA_HEAD_DIM - ROT_DIM), F32)
    zeros = jnp.zeros((T, DA_HEAD_DIM - ROT_DIM), F32)
    z8 = jnp.zeros((T, ROT_HALF), F32)
    cos64 = jnp.concatenate([cos, cos, ones], axis=1)
    sa64 = jnp.concatenate([z8, sin, zeros], axis=1)
    sb64 = jnp.concatenate([-sin, z8, zeros], axis=1)
    rep = LANES_V7X // DA_HEAD_DIM
    return (jnp.tile(cos64, (1, rep)), jnp.tile(sa64, (1, rep)), jnp.tile(sb64, (1, rep)))


def _trunk(x, mem, w):
    B, T, _ = x.shape
    M = B * T
    x2d = x.reshape(M, D_MODEL)
    cos_t, sa_t, sb_t = _rope_tables(T)
    qk, vt, xc = _proj(x2d, w["w_qkvx"], cos_t, sa_t, sb_t, w["conv_w"], w["conv_b"], T)
    hf, hb = _lru(xc.reshape(B, T, D_MODEL), w["w_gates"], w["b_gates"], w["lru_a"])
    attn = _attention(qk, vt, w["lam4"], w["subln_g"])
    x1 = _merge(attn.reshape(M, D_MODEL), hf.reshape(M, D_MODEL), hb.reshape(M, D_MODEL), x2d,
                w["w_gate"], w["p_attn"], w["p_lru"], w["w_mix_out"], w["ln1_g"], w["ln1_b"])
    kv = _memkv(mem.reshape(B * N_MEM, D_MODEL), w["xa_wkv"])
    y = _tail(x1, kv.reshape(B, N_MEM, 2 * D_MODEL), T, w["xa_wq"], w["xa_wo"], w["ln2_g"],
              w["ln2_b"], w["ffn_w_in"], w["ffn_w_out"], w["ln3_g"], w["ln3_b"])
    return y.reshape(B, T, D_MODEL)


def _prepare_weights(w_in, lambda_q1, lambda_k1, lambda_q2, lambda_k2, subln_g, conv_w, conv_b,
                     lru_wa, lru_ba, lru_wx, lru_bx, lru_a, p_attn, p_lru, w_mix_out, ln1_g, ln1_b,
                     xa_wq, xa_wkv, xa_wo, ln2_g, ln2_b, ffn_w_in, ffn_w_out, ln3_g, ln3_b):
    row = lambda v: v[0].reshape(1, -1).astype(F32)
    return {
        "w_qkvx": w_in[0, :, :4 * D_MODEL].astype(BF16),
        "w_gate": w_in[0, :, 4 * D_MODEL:].astype(BF16),
        "lam4": jnp.stack([lambda_q1[0], lambda_k1[0], lambda_q2[0], lambda_k2[0]]).astype(F32),
        "subln_g": row(subln_g),
        "conv_w": conv_w[0].astype(F32),
        "conv_b": row(conv_b),
        "w_gates": jnp.concatenate([lru_wa[0], lru_wx[0]], axis=-1).astype(BF16),
        "b_gates": jnp.concatenate([lru_ba[0], lru_bx[0]], axis=-1).astype(F32),
        "lru_a": lru_a[0].astype(F32),
        "p_attn": p_attn[0].astype(BF16),
        "p_lru": p_lru[0].astype(BF16),
        "w_mix_out": w_mix_out[0].astype(BF16),
        "ln1_g": row(ln1_g), "ln1_b": row(ln1_b),
        "xa_wq": xa_wq[0].astype(BF16),
        "xa_wkv": xa_wkv[0].astype(BF16),
        "xa_wo": xa_wo[0].astype(BF16),
        "ln2_g": row(ln2_g), "ln2_b": row(ln2_b),
        "ffn_w_in": ffn_w_in[0].astype(BF16),
        "ffn_w_out": ffn_w_out[0].astype(BF16),
        "ln3_g": row(ln3_g), "ln3_b": row(ln3_b),
    }


def kernel(x_prompt, x_sample, mem_prompt, mem_sample, w_in, lambda_q1, lambda_k1, lambda_q2, lambda_k2, subln_g, conv_w, conv_b, lru_wa, lru_ba, lru_wx, lru_bx, lru_a, p_attn, p_lru, w_mix_out, ln1_g, ln1_b, xa_wq, xa_wkv, xa_wo, ln2_g, ln2_b, ffn_w_in, ffn_w_out, ln3_g, ln3_b):
    w = _prepare_weights(w_in, lambda_q1, lambda_k1, lambda_q2, lambda_k2, subln_g, conv_w, conv_b,
                         lru_wa, lru_ba, lru_wx, lru_bx, lru_a, p_attn, p_lru, w_mix_out, ln1_g,
                         ln1_b, xa_wq, xa_wkv, xa_wo, ln2_g, ln2_b, ffn_w_in, ffn_w_out, ln3_g, ln3_b)
    return (_trunk(x_prompt, mem_prompt, w), _trunk(x_sample, mem_sample, w))
```
